```python
import jax, jax.numpy as jnp
from jax import lax
import numpy as np


D_MODEL = 1024
BATCH = 4
SEQ = 4096
DEPTH = 4

CHUNK = 64
Q_BLOCK = 128
NORM_EPS = 1e-6
D_FF = 256 * ((8 * D_MODEL // 3 + 255) // 256)

D_RNN = D_MODEL
RG_BLOCKS = 16
RG_BLOCK_W = D_RNN // RG_BLOCKS
CONV_W = 4
RG_C = 8.0

SB_HEAD_DIM = 128
SB_HEADS = D_MODEL // 128
SB_W = SB_HEADS * SB_HEAD_DIM

MLA_HEADS = 8
MLA_Q_LORA = D_MODEL // 4
MLA_KV_LORA = D_MODEL // 4
MLA_NOPE = 128
MLA_ROPE = 64
MLA_V = 128
MLA_V_W = MLA_HEADS * MLA_V
ROPE_THETA = 10000.0

N_BRANCHES = 3
IN_SPLITS = (D_RNN, D_RNN, SB_W, SB_W, SB_W, MLA_Q_LORA, MLA_KV_LORA, MLA_ROPE, N_BRANCHES * D_MODEL)
N_IN = D_RNN * 2 + SB_W * 3 + MLA_Q_LORA + MLA_KV_LORA + MLA_ROPE + N_BRANCHES * D_MODEL

kernel_name = 'hybrid_rglru_stickbreak_mla_macaron_trunk'


def rms_norm(x, g):
    xf = x.astype(jnp.float32)
    y = xf * lax.rsqrt(jnp.mean(xf * xf, axis=-1, keepdims=True) + NORM_EPS)
    return (y * g.astype(jnp.float32)).astype(x.dtype)


def swiglu_ffn(h, w_gate_up, w_down):
    gate, up = jnp.split(h @ w_gate_up, 2, axis=-1)
    return (jax.nn.silu(gate) * up) @ w_down


def causal_depthwise_conv(u, w, b):
    S = u.shape[1]
    up = jnp.pad(u, ((0, 0), (CONV_W - 1, 0), (0, 0)))
    out = b
    for tap in range(CONV_W):
        out = out + up[:, tap:tap + S] * w[tap]
    return out


def _lin_rec_combine(earlier, later):
    a1, b1 = earlier
    a2, b2 = later
    return a1 * a2, a2 * b1 + b2


def rg_lru(u, w_a, b_a, w_x, b_x, lam):
    B, S, _ = u.shape
    ub = u.reshape(B, S, RG_BLOCKS, RG_BLOCK_W)
    r = jax.nn.sigmoid(jnp.einsum('bsnj,njk->bsnk', ub, w_a).reshape(B, S, D_RNN) + b_a)
    i = jax.nn.sigmoid(jnp.einsum('bsnj,njk->bsnk', ub, w_x).reshape(B, S, D_RNN) + b_x)
    log_a = (-RG_C * jax.nn.softplus(-lam.astype(jnp.float32))) * r.astype(jnp.float32)
    a = jnp.exp(log_a)
    b = jnp.sqrt(-jnp.expm1(2.0 * log_a)) * (i * u).astype(jnp.float32)
    _, h = lax.associative_scan(_lin_rec_combine, (a, b), axis=1)
    return h.astype(u.dtype)


def stick_breaking_attention(q, k, v):
    S = q.shape[1]
    scale = SB_HEAD_DIM ** -0.5
    outs = []
    for blk in range(S // Q_BLOCK):
        q0 = blk * Q_BLOCK
        k_end = q0 + Q_BLOCK
        z = jnp.einsum('bqhd,bkhd->bhqk', q[:, q0:k_end], k[:, :k_end]).astype(jnp.float32) * scale
        q_pos = q0 + jnp.arange(Q_BLOCK)
        k_pos = jnp.arange(k_end)
        earlier = k_pos[None, :] < q_pos[:, None]
        log_keep = jnp.where(earlier, jax.nn.log_sigmoid(-z), 0.0)
        between = lax.cumsum(log_keep, axis=3, reverse=True) - log_keep
        w = jnp.where(earlier, jnp.exp(jax.nn.log_sigmoid(z) + between), 0.0)
        outs.append(jnp.einsum('bhqk,bkhd->bqhd', w.astype(v.dtype), v[:, :k_end]))
    return jnp.concatenate(outs, axis=1)


def chunk_causal_softmax_attention(q, k, v, scale):
    S = q.shape[1]
    outs = []
    for blk in range(S // Q_BLOCK):
        q0 = blk * Q_BLOCK
        k_end = q0 + Q_BLOCK
        s = jnp.einsum('bqhd,bkhd->bhqk', q[:, q0:k_end], k[:, :k_end]).astype(jnp.float32) * scale
        q_chunk = (q0 + jnp.arange(Q_BLOCK)) // CHUNK
        k_chunk = jnp.arange(k_end) // CHUNK
        s = jnp.where(k_chunk[None, :] <= q_chunk[:, None], s, -jnp.inf)
        p = jax.nn.softmax(s, axis=-1).astype(v.dtype)
        outs.append(jnp.einsum('bhqk,bkhd->bqhd', p, v[:, :k_end]))
    return jnp.concatenate(outs, axis=1)


def rope_tables(positions, dtype):
    inv = ROPE_THETA ** (-jnp.arange(0, MLA_ROPE, 2, dtype=jnp.float32) / MLA_ROPE)
    ang = positions.astype(jnp.float32)[..., None] * inv
    return jnp.cos(ang).astype(dtype), jnp.sin(ang).astype(dtype)


def apply_rope(x, cos, sin):
    x1, x2 = jnp.split(x, 2, axis=-1)
    return jnp.concatenate([x1 * cos - x2 * sin, x2 * cos + x1 * sin], axis=-1)


def mla_branch(c_q_raw, c_kv_raw, k_rope_raw, q_norm, w_uq, kv_norm, w_ukv, cos, sin):
    B, S, _ = c_q_raw.shape
    q = (rms_norm(c_q_raw, q_norm) @ w_uq).reshape(B, S, MLA_HEADS, MLA_NOPE + MLA_ROPE)
    q_nope, q_rope = jnp.split(q, [MLA_NOPE], axis=-1)
    q_rope = apply_rope(q_rope, cos[:, :, None, :], sin[:, :, None, :])
    kv = (rms_norm(c_kv_raw, kv_norm) @ w_ukv).reshape(B, S, MLA_HEADS, MLA_NOPE + MLA_V)
    k_nope, v = jnp.split(kv, [MLA_NOPE], axis=-1)
    k_rope = apply_rope(k_rope_raw, cos, sin)
    k = jnp.concatenate([k_nope, jnp.broadcast_to(k_rope[:, :, None, :], (B, S, MLA_HEADS, MLA_ROPE))], axis=-1)
    q = jnp.concatenate([q_nope, q_rope], axis=-1)
    o = chunk_causal_softmax_attention(q, k, v, (MLA_NOPE + MLA_ROPE) ** -0.5)
    return o.reshape(B, S, MLA_V_W)


def setup_inputs(seed: int = 0) -> dict:
    key = jax.random.key(seed)
    ks = iter(jax.random.split(key, 48))
    L, D = DEPTH, D_MODEL

    def normal(shape, fan_in):
        return jax.random.normal(next(ks), shape, jnp.float32) * (fan_in ** -0.5)

    def gain(shape):
        return 1.0 + 0.02 * jax.random.normal(next(ks), shape, jnp.float32)

    def bias(shape):
        return 0.02 * jax.random.normal(next(ks), shape, jnp.float32)

    x = jax.random.normal(next(ks), (BATCH, SEQ, D), jnp.float32)
    offsets = jax.random.randint(next(ks), (BATCH, 1), 0, 16384, dtype=jnp.int32)
    positions = (offsets + jnp.arange(SEQ, dtype=jnp.int32)[None, :]).astype(jnp.int32)
    a0 = jax.random.uniform(next(ks), (L, D_RNN), jnp.float32, minval=0.9, maxval=0.999)
    base = a0 ** (1.0 / RG_C)
    rg_lambda = jnp.log(base) - jnp.log1p(-base)
    return {
        'x': x,
        'positions': positions,
        'ffn1_norm': gain((L, D)),
        'ffn1_w_gate_up': normal((L, D, 2 * D_FF), D),
        'ffn1_w_down': normal((L, D_FF, D), D_FF),
        'mix_norm': gain((L, D)),
        'w_in': normal((L, D, N_IN), D),
        'conv_w': normal((L, CONV_W, D_RNN), CONV_W),
        'conv_b': bias((L, D_RNN)),
        'rg_w_a': normal((L, RG_BLOCKS, RG_BLOCK_W, RG_BLOCK_W), RG_BLOCK_W),
        'rg_b_a': bias((L, D_RNN)),
        'rg_w_x': normal((L, RG_BLOCKS, RG_BLOCK_W, RG_BLOCK_W), RG_BLOCK_W),
        'rg_b_x': bias((L, D_RNN)),
        'rg_lambda': rg_lambda,
        'mla_q_norm': gain((L, MLA_Q_LORA)),
        'mla_w_uq': normal((L, MLA_Q_LORA, MLA_HEADS * (MLA_NOPE + MLA_ROPE)), MLA_Q_LORA),
        'mla_kv_norm': gain((L, MLA_KV_LORA)),
        'mla_w_ukv': normal((L, MLA_KV_LORA, MLA_HEADS * (MLA_NOPE + MLA_V)), MLA_KV_LORA),
        'w_branch_a': normal((L, D_RNN, D), D_RNN),
        'w_branch_b': normal((L, SB_W, D), SB_W),
        'w_branch_c': normal((L, MLA_V_W, D), MLA_V_W),
        'w_out': normal((L, D, D), D),
        'ffn2_norm': gain((L, D)),
        'ffn2_w_gate_up': normal((L, D, 2 * D_FF), D),
        'ffn2_w_down': normal((L, D_FF, D), D_FF),
        'final_norm': gain((D,)),
    }


def reference(x, positions, ffn1_norm, ffn1_w_gate_up, ffn1_w_down, mix_norm, w_in,
              conv_w, conv_b, rg_w_a, rg_b_a, rg_w_x, rg_b_x, rg_lambda,
              mla_q_norm, mla_w_uq, mla_kv_norm, mla_w_ukv,
              w_branch_a, w_branch_b, w_branch_c, w_out,
              ffn2_norm, ffn2_w_gate_up, ffn2_w_down, final_norm):
    B, S, _ = x.shape
    split_points = []
    acc = 0
    for width in IN_SPLITS[:-1]:
        acc += width
        split_points.append(acc)
    cos, sin = rope_tables(positions, x.dtype)

    for l in range(DEPTH):
        x = x + 0.5 * swiglu_ffn(rms_norm(x, ffn1_norm[l]), ffn1_w_gate_up[l], ffn1_w_down[l])

        h = rms_norm(x, mix_norm[l])
        rg_x, rg_g, sb_q, sb_k, sb_v, c_q, c_kv, k_r, gate_logits = jnp.split(h @ w_in[l], split_points, axis=-1)

        u = causal_depthwise_conv(rg_x, conv_w[l], conv_b[l])
        y_a = rg_lru(u, rg_w_a[l], rg_b_a[l], rg_w_x[l], rg_b_x[l], rg_lambda[l]) * jax.nn.gelu(rg_g)

        hd = (B, S, SB_HEADS, SB_HEAD_DIM)
        y_b = stick_breaking_attention(sb_q.reshape(hd), sb_k.reshape(hd), sb_v.reshape(hd)).reshape(B, S, SB_W)

        y_c = mla_branch(c_q, c_kv, k_r, mla_q_norm[l], mla_w_uq[l], mla_kv_norm[l], mla_w_ukv[l], cos, sin)

        g_a, g_b, g_c = jnp.split(jax.nn.sigmoid(gate_logits), N_BRANCHES, axis=-1)
        merged = g_a * (y_a @ w_branch_a[l]) + g_b * (y_b @ w_branch_b[l]) + g_c * (y_c @ w_branch_c[l])
        x = x + merged @ w_out[l]

        x = x + 0.5 * swiglu_ffn(rms_norm(x, ffn2_norm[l]), ffn2_w_gate_up[l], ffn2_w_down[l])

    return rms_norm(x, final_norm)
```

```python
import functools
import math

import jax
import jax.numpy as jnp
from jax import lax
from jax.experimental import pallas as pl
from jax.experimental.pallas import tpu as pltpu

F32 = jnp.float32
BF16 = jnp.bfloat16

NORM_EPS = 1e-6
CHUNK = 64
RG_C = 8.0
CONV_W = 4
SB_HEAD_DIM = 128
MLA_HEADS = 8
MLA_NOPE = 128
MLA_ROPE = 64
MLA_V = 128
MLA_QK_PAD = 256
ROPE_THETA = 10000.0

V7X_VMEM_LIMIT_BYTES = 56 * 1024 * 1024
RG_GROUP = 256
SB_DEAD_LOG = -105.0

_DN_NT = (((1,), (1,)), ((), ()))


def _cparams(sem):
    return pltpu.CompilerParams(dimension_semantics=sem,
                                vmem_limit_bytes=V7X_VMEM_LIMIT_BYTES)


def _const_spec(shape):
    nd = len(shape)
    return pl.BlockSpec(shape, lambda *_: (0,) * nd, pipeline_mode=pl.Buffered(1))


def _rms(x, g):
    return x * lax.rsqrt(jnp.mean(x * x, axis=-1, keepdims=True) + NORM_EPS) * g


def _rope_kernel(pos_ref, inv_ref, cs_ref, sn_ref):
    ang = pos_ref[...] * inv_ref[...]
    lane = lax.broadcasted_iota(jnp.int32, ang.shape, 1)
    half = MLA_ROPE // 2
    c = jnp.cos(ang)
    s = jnp.sin(ang)
    in_rope = (lane >= MLA_NOPE) & (lane < MLA_NOPE + MLA_ROPE)
    cs_ref[...] = jnp.where(lane < MLA_NOPE, 1.0, jnp.where(in_rope, c, 0.0))
    sn_ref[...] = jnp.where(in_rope, jnp.where(lane < MLA_NOPE + half, -s, s), 0.0)


def _rope_tables(positions, tm):
    T = positions.size
    pos = positions.reshape(T, 1).astype(F32)
    inv = ROPE_THETA ** (-jnp.arange(0, MLA_ROPE, 2, dtype=F32) / MLA_ROPE)
    inv_pad = jnp.concatenate([jnp.zeros((MLA_NOPE,), F32), inv, inv,
                               jnp.zeros((MLA_QK_PAD - MLA_NOPE - MLA_ROPE,), F32)])[None, :]
    return pl.pallas_call(
        _rope_kernel,
        grid=(T // tm,),
        in_specs=[pl.BlockSpec((tm, 1), lambda i: (i, 0)),
                  pl.BlockSpec((1, MLA_QK_PAD), lambda i: (0, 0))],
        out_specs=[pl.BlockSpec((tm, MLA_QK_PAD), lambda i: (i, 0))] * 2,
        out_shape=[jax.ShapeDtypeStruct((T, MLA_QK_PAD), F32)] * 2,
        compiler_params=_cparams(("parallel",)),
        name="rope_tables",
    )(pos, inv_pad)


def _ffn_kernel(x_ref, g_ref, wgu_ref, wd_ref, *rest, d_ff, n_chunk, final):
    if final:
        fg_ref, o_ref, a_ref = rest
    else:
        o_ref, a_ref = rest
    x = x_ref[...]
    h = _rms(x, g_ref[...]).astype(BF16)
    tf = d_ff // n_chunk
    for c in range(n_chunk):
        gate = jnp.dot(h, wgu_ref[:, c * tf:(c + 1) * tf], preferred_element_type=F32)
        up = jnp.dot(h, wgu_ref[:, d_ff + c * tf:d_ff + (c + 1) * tf], preferred_element_type=F32)
        a_ref[:, c * tf:(c + 1) * tf] = (gate * jax.nn.sigmoid(gate) * up).astype(BF16)
    y = x + 0.5 * jnp.dot(a_ref[...], wd_ref[...], preferred_element_type=F32)
    if final:
        y = _rms(y, fg_ref[...])
    o_ref[...] = y


def _ffn(x, g, wgu, wd, final_g=None, *, tm):
    T, D = x.shape
    d_ff = wd.shape[0]
    n_chunk = 2 if (d_ff // 2) % 128 == 0 else 1
    final = final_g is not None
    in_specs = [pl.BlockSpec((tm, D), lambda i: (i, 0)),
                _const_spec((1, D)),
                _const_spec((D, 2 * d_ff)),
                _const_spec((d_ff, D))]
    args = [x, g, wgu, wd]
    if final:
        in_specs.append(_const_spec((1, D)))
        args.append(final_g)
    return pl.pallas_call(
        functools.partial(_ffn_kernel, d_ff=d_ff, n_chunk=n_chunk, final=final),
        grid=(T // tm,),
        in_specs=in_specs,
        out_specs=pl.BlockSpec((tm, D), lambda i: (i, 0)),
        out_shape=jax.ShapeDtypeStruct((T, D), F32),
        scratch_shapes=[pltpu.VMEM((tm, d_ff), BF16)],
        compiler_params=_cparams(("parallel",)),
        name="ffn_final" if final else "ffn",
    )(*args)


def _inproj_kernel(x_ref, g_ref, w_ref, cs_ref, sn_ref,
                   rgx_ref, rgg_ref, q_ref, k_ref, v_ref, cq_ref, ckv_ref, gate_ref, kr_ref,
                   *, d, lora, sb_scale):
    h = _rms(x_ref[...], g_ref[...]).astype(BF16)

    def proj(lo, width):
        return jnp.dot(h, w_ref[:, lo:lo + width], preferred_element_type=F32)

    o = 0
    rgx_ref[...] = proj(o, d); o += d
    rgg_ref[...] = proj(o, d); o += d
    q_ref[...] = (proj(o, d) * sb_scale).astype(BF16); o += d
    k_ref[...] = proj(o, d).astype(BF16); o += d
    v_ref[...] = proj(o, d).astype(BF16); o += d
    cq_ref[...] = proj(o, lora); o += lora
    ckv_ref[...] = proj(o, lora); o += lora
    gate_ref[...] = jax.nn.sigmoid(proj(o, 3 * d)); o += 3 * d
    kr = proj(o, MLA_QK_PAD); o += MLA_QK_PAD
    kr_sw = proj(o, MLA_QK_PAD)
    kr_ref[...] = kr * cs_ref[...] + kr_sw * sn_ref[...]


def _inproj(x, g, w, cs, sn, *, tm, lora):
    T, D = x.shape
    n_cols = w.shape[1]
    row = lambda width: pl.BlockSpec((tm, width), lambda i: (i, 0))
    sds = lambda width, dt: jax.ShapeDtypeStruct((T, width), dt)
    return pl.pallas_call(
        functools.partial(_inproj_kernel, d=D, lora=lora, sb_scale=SB_HEAD_DIM ** -0.5),
        grid=(T // tm,),
        in_specs=[row(D), _const_spec((1, D)), _const_spec((D, n_cols)),
                  row(MLA_QK_PAD), row(MLA_QK_PAD)],
        out_specs=[row(D), row(D), row(D), row(D), row(D), row(lora), row(lora),
                   row(3 * D), row(MLA_QK_PAD)],
        out_shape=[sds(D, F32), sds(D, F32), sds(D, BF16), sds(D, BF16), sds(D, BF16),
                   sds(lora, F32), sds(lora, F32), sds(3 * D, F32), sds(MLA_QK_PAD, F32)],
        compiler_params=_cparams(("parallel",)),
        name="in_proj",
    )(x, g, w, cs, sn)


def _rglru_kernel(x_ref, gate_ref, cw_ref, cb_ref, wa_ref, ba_ref, wx_ref, bx_ref, lam_ref,
                  o_ref, xbuf, hprev, *, ts, n_group):
    s = pl.program_id(1)
    halo = 8

    @pl.when(s == 0)
    def _():
        xbuf[0:halo, :] = jnp.zeros((halo, xbuf.shape[1]), F32)
        hprev[...] = jnp.zeros(hprev.shape, F32)

    xbuf[halo:halo + ts, :] = x_ref[...]
    row = lax.broadcasted_iota(jnp.int32, (ts, RG_GROUP), 0)

    for gi in range(n_group):
        sl = slice(gi * RG_GROUP, (gi + 1) * RG_GROUP)
        u = cb_ref[:, sl] + cw_ref[CONV_W - 1:CONV_W, sl] * xbuf[halo:halo + ts, sl]
        for back in range(1, CONV_W):
            tap = CONV_W - 1 - back
            u = u + cw_ref[tap:tap + 1, sl] * xbuf[halo - back:halo - back + ts, sl]
        ub = u.astype(BF16)
        r = jax.nn.sigmoid(jnp.dot(ub, wa_ref[gi], preferred_element_type=F32) + ba_ref[:, sl])
        i_gate = jax.nn.sigmoid(jnp.dot(ub, wx_ref[gi], preferred_element_type=F32) + bx_ref[:, sl])
        lam = lam_ref[:, sl]
        neg_softplus = -(jnp.maximum(-lam, 0.0) + jnp.log1p(jnp.exp(-jnp.abs(lam))))
        log_a = (RG_C * neg_softplus) * r
        a = jnp.exp(log_a)
        b = jnp.sqrt(-jnp.tanh(log_a) * (a * a + 1.0)) * (i_gate * u)
        d = 1
        while d < ts:
            m = row >= d
            a_s = pltpu.roll(a, d, 0)
            b_s = pltpu.roll(b, d, 0)
            b = jnp.where(m, a * b_s + b, b)
            a = jnp.where(m, a * a_s, a)
            d *= 2
        hs = a * hprev[:, sl] + b
        hprev[:, sl] = hs[ts - 1:ts, :]
        o_ref[:, sl] = (hs * jax.nn.gelu(gate_ref[:, sl], approximate=True)).astype(o_ref.dtype)

    xbuf[0:halo, :] = x_ref[ts - halo:ts, :]


def _rglru(rg_x, rg_g, cw, cb, wa, ba, wx, bx, lam, *, batch, ts):
    T, C = rg_x.shape
    S = T // batch
    ns = S // ts
    n_group = C // RG_GROUP
    row = pl.BlockSpec((ts, C), lambda b, s: (b * ns + s, 0))
    return pl.pallas_call(
        functools.partial(_rglru_kernel, ts=ts, n_group=n_group),
        grid=(batch, ns),
        in_specs=[row, row,
                  _const_spec((CONV_W, C)), _const_spec((1, C)),
                  _const_spec((n_group, RG_GROUP, RG_GROUP)), _const_spec((1, C)),
                  _const_spec((n_group, RG_GROUP, RG_GROUP)), _const_spec((1, C)),
                  _const_spec((1, C))],
        out_specs=row,
        out_shape=jax.ShapeDtypeStruct((T, C), BF16),
        scratch_shapes=[pltpu.VMEM((ts + 8, C), F32), pltpu.VMEM((1, C), F32)],
        compiler_params=_cparams(("parallel", "arbitrary")),
        name="rglru",
    )(rg_x, rg_g, cw, cb, wa, ba, wx, bx, lam)


def _sb_kernel(q_ref, k_ref, v_ref, tri_ref, o_ref, *, blk, nblk):
    rowi = lax.broadcasted_iota(jnp.int32, (blk, blk), 0)
    coli = lax.broadcasted_iota(jnp.int32, (blk, blk), 1)
    earlier = coli < rowi

    def block(q, j, acc, rem, diag):
        k0 = pl.multiple_of(j * blk, blk)
        k = k_ref[pl.ds(k0, blk), :]
        v = v_ref[pl.ds(k0, blk), :]
        z = lax.dot_general(q, k, _DN_NT, preferred_element_type=F32)
        softplus = jnp.maximum(z, 0.0) + jnp.log(1.0 + jnp.exp(-jnp.abs(z)))
        log_keep = -softplus
        if diag:
            log_keep = jnp.where(earlier, log_keep, 0.0)
        hi = log_keep.astype(BF16)
        lo = (log_keep - hi.astype(F32)).astype(BF16)
        tri = tri_ref[...]
        between = (jnp.dot(hi, tri, preferred_element_type=F32)
                   + jnp.dot(lo, tri, preferred_element_type=F32))
        w = jnp.exp((z - softplus) + between + rem)
        if diag:
            w = jnp.where(earlier, w, 0.0)
        acc = acc + jnp.dot(w.astype(BF16), v, preferred_element_type=F32)
        rem = rem + jnp.sum(log_keep, axis=1, keepdims=True)
        return acc, rem

    def q_body(i, carry):
        q0 = pl.multiple_of(i * blk, blk)
        q = q_ref[pl.ds(q0, blk), :]
        acc, rem = block(q, i, jnp.zeros((blk, SB_HEAD_DIM), F32), jnp.zeros((blk, 1), F32), True)

        def cond(c):
            j, live, _, _ = c
            return jnp.logical_and(j >= 0, live)

        def body(c):
            j, _, acc, rem = c
            acc, rem = block(q, j, acc, rem, False)
            return j - 1, jnp.max(rem) > SB_DEAD_LOG, acc, rem

        _, _, acc, _ = lax.while_loop(cond, body, (i - 1, jnp.max(rem) > SB_DEAD_LOG, acc, rem))
        o_ref[pl.ds(q0, blk), :] = acc.astype(o_ref.dtype)
        return carry

    lax.fori_loop(0, nblk, q_body, 0)


def _sb_attention(q, k, v, *, batch, blk):
    T, W = q.shape
    S = T // batch
    heads = W // SB_HEAD_DIM
    tri = (lax.broadcasted_iota(jnp.int32, (blk, blk), 0)
           > lax.broadcasted_iota(jnp.int32, (blk, blk), 1)).astype(BF16)
    spec = pl.BlockSpec((S, SB_HEAD_DIM), lambda b, h: (b, h))
    return pl.pallas_call(
        functools.partial(_sb_kernel, blk=blk, nblk=S // blk),
        grid=(batch, heads),
        in_specs=[spec, spec, spec, _const_spec((blk, blk))],
        out_specs=spec,
        out_shape=jax.ShapeDtypeStruct((T, W), BF16),
        compiler_params=_cparams(("parallel", "parallel")),
        name="sb_attention",
    )(q, k, v, tri)


def _mla_prep_kernel(cq_ref, ckv_ref, kr_ref, cs_ref, sn_ref, qn_ref, kvn_ref,
                     wq_ref, wqs_ref, wk_ref, wv_ref, q_ref, k_ref, v_ref, *, scale):
    cq = _rms(cq_ref[...], qn_ref[...]).astype(BF16)
    ckv = _rms(ckv_ref[...], kvn_ref[...]).astype(BF16)
    cs = cs_ref[...]
    sn = sn_ref[...]
    kr = kr_ref[...]
    v_ref[...] = jnp.dot(ckv, wv_ref[...], preferred_element_type=F32).astype(BF16)
    for h in range(MLA_HEADS):
        sl = slice(h * MLA_QK_PAD, (h + 1) * MLA_QK_PAD)
        qh = jnp.dot(cq, wq_ref[:, sl], preferred_element_type=F32)
        qh_sw = jnp.dot(cq, wqs_ref[:, sl], preferred_element_type=F32)
        q_ref[:, sl] = ((qh * cs + qh_sw * sn) * scale).astype(BF16)
        kh = jnp.dot(ckv, wk_ref[:, sl], preferred_element_type=F32)
        k_ref[:, sl] = (kh + kr).astype(BF16)


def _mla_prep(cq, ckv, kr, cs, sn, qn, kvn, wq, wqs, wk, wv, *, tm):
    T, lora = cq.shape
    qk_w = MLA_HEADS * MLA_QK_PAD
    v_w = MLA_HEADS * MLA_V
    row = lambda width: pl.BlockSpec((tm, width), lambda i: (i, 0))
    return pl.pallas_call(
        functools.partial(_mla_prep_kernel, scale=(MLA_NOPE + MLA_ROPE) ** -0.5),
        grid=(T // tm,),
        in_specs=[row(lora), row(lora), row(MLA_QK_PAD), row(MLA_QK_PAD), row(MLA_QK_PAD),
                  _const_spec((1, lora)), _const_spec((1, lora)),
                  _const_spec((lora, qk_w)), _const_spec((lora, qk_w)),
                  _const_spec((lora, qk_w)), _const_spec((lora, v_w))],
        out_specs=[row(qk_w), row(qk_w), row(v_w)],
        out_shape=[jax.ShapeDtypeStruct((T, qk_w), BF16),
                   jax.ShapeDtypeStruct((T, qk_w), BF16),
                   jax.ShapeDtypeStruct((T, v_w), BF16)],
        compiler_params=_cparams(("parallel",)),
        name="mla_prep",
    )(cq, ckv, kr, cs, sn, qn, kvn, wq, wqs, wk, wv)


def _mla_kernel(q_ref, k_ref, v_ref, o_ref, *, blk, nblk):
    rowc = lax.broadcasted_iota(jnp.int32, (blk, blk), 0) // CHUNK
    colc = lax.broadcasted_iota(jnp.int32, (blk, blk), 1) // CHUNK
    visible = colc <= rowc

    def q_body(i, carry):
        q0 = pl.multiple_of(i * blk, blk)
        q = q_ref[pl.ds(q0, blk), :]

        s = lax.dot_general(q, k_ref[pl.ds(q0, blk), :], _DN_NT, preferred_element_type=F32)
        s = jnp.where(visible, s, -jnp.inf)
        m = jnp.max(s, axis=1, keepdims=True)
        p = jnp.exp(s - m)
        l = jnp.sum(p, axis=1, keepdims=True)
        acc = jnp.dot(p.astype(BF16), v_ref[pl.ds(q0, blk), :], preferred_element_type=F32)

        def kv_body(j, c):
            m, l, acc = c
            k0 = pl.multiple_of(j * blk, blk)
            s = lax.dot_general(q, k_ref[pl.ds(k0, blk), :], _DN_NT, preferred_element_type=F32)
            m_new = jnp.maximum(m, jnp.max(s, axis=1, keepdims=True))
            alpha = jnp.exp(m - m_new)
            p = jnp.exp(s - m_new)
            l = alpha * l + jnp.sum(p, axis=1, keepdims=True)
            acc = alpha * acc + jnp.dot(p.astype(BF16), v_ref[pl.ds(k0, blk), :],
                                        preferred_element_type=F32)
            return m_new, l, acc

        m, l, acc = lax.fori_loop(0, i, kv_body, (m, l, acc))
        o_ref[pl.ds(q0, blk), :] = (acc / l).astype(o_ref.dtype)
        return carry

    lax.fori_loop(0, nblk, q_body, 0)


def _mla_attention(q, k, v, *, batch, blk):
    T = q.shape[0]
    S = T // batch
    qk_spec = pl.BlockSpec((S, MLA_QK_PAD), lambda b, h: (b, h))
    v_spec = pl.BlockSpec((S, MLA_V), lambda b, h: (b, h))
    return pl.pallas_call(
        functools.partial(_mla_kernel, blk=blk, nblk=S // blk),
        grid=(batch, MLA_HEADS),
        in_specs=[qk_spec, qk_spec, v_spec],
        out_specs=v_spec,
        out_shape=jax.ShapeDtypeStruct((T, MLA_HEADS * MLA_V), BF16),
        compiler_params=_cparams(("parallel", "parallel")),
        name="mla_attention",
    )(q, k, v)


def _merge_kernel(x_ref, ya_ref, yb_ref, yc_ref, gate_ref, wa_ref, wb_ref, wc_ref, wo_ref, o_ref, *, d):
    merged = gate_ref[:, 0:d] * jnp.dot(ya_ref[...], wa_ref[...], preferred_element_type=F32)
    merged = merged + gate_ref[:, d:2 * d] * jnp.dot(yb_ref[...], wb_ref[...], preferred_element_type=F32)
    merged = merged + gate_ref[:, 2 * d:3 * d] * jnp.dot(yc_ref[...], wc_ref[...], preferred_element_type=F32)
    o_ref[...] = x_ref[...] + jnp.dot(merged.astype(BF16), wo_ref[...], preferred_element_type=F32)


def _merge(x, ya, yb, yc, gates, wa, wb, wc, wo, *, tm):
    T, D = x.shape
    row = lambda width: pl.BlockSpec((tm, width), lambda i: (i, 0))
    wspec = _const_spec((D, D))
    return pl.pallas_call(
        functools.partial(_merge_kernel, d=D),
        grid=(T // tm,),
        in_specs=[row(D), row(D), row(D), row(D), row(3 * D), wspec, wspec, wspec, wspec],
        out_specs=row(D),
        out_shape=jax.ShapeDtypeStruct((T, D), F32),
        compiler_params=_cparams(("parallel",)),
        name="merge",
    )(x, ya, yb, yc, gates, wa, wb, wc, wo)


def _prep_w_in(w_in, d, lora):
    kr_lo = 5 * d + 2 * lora
    kr_hi = kr_lo + MLA_ROPE
    half = MLA_ROPE // 2
    kr = w_in[:, :, kr_lo:kr_hi]
    lead = w_in.shape[:2]
    z_nope = jnp.zeros(lead + (MLA_NOPE,), w_in.dtype)
    z_tail = jnp.zeros(lead + (MLA_QK_PAD - MLA_NOPE - MLA_ROPE,), w_in.dtype)
    kr_pad = jnp.concatenate([z_nope, kr[..., :half], kr[..., half:], z_tail], axis=-1)
    kr_swp = jnp.concatenate([z_nope, kr[..., half:], kr[..., :half], z_tail], axis=-1)
    return jnp.concatenate([w_in[:, :, :kr_lo], w_in[:, :, kr_hi:], kr_pad, kr_swp], axis=-1).astype(BF16)


def _prep_w_uq(w_uq):
    L, lora, _ = w_uq.shape
    half = MLA_ROPE // 2
    w = w_uq.reshape(L, lora, MLA_HEADS, MLA_NOPE + MLA_ROPE)
    nope, r1, r2 = w[..., :MLA_NOPE], w[..., MLA_NOPE:MLA_NOPE + half], w[..., MLA_NOPE + half:]
    z_tail = jnp.zeros(w.shape[:3] + (MLA_QK_PAD - MLA_NOPE - MLA_ROPE,), w.dtype)
    plain = jnp.concatenate([nope, r1, r2, z_tail], axis=-1)
    swapped = jnp.concatenate([jnp.zeros_like(nope), r2, r1, z_tail], axis=-1)
    shape = (L, lora, MLA_HEADS * MLA_QK_PAD)
    return plain.reshape(shape).astype(BF16), swapped.reshape(shape).astype(BF16)


def _prep_w_ukv(w_ukv):
    L, lora, _ = w_ukv.shape
    w = w_ukv.reshape(L, lora, MLA_HEADS, MLA_NOPE + MLA_V)
    k_nope, v = w[..., :MLA_NOPE], w[..., MLA_NOPE:]
    z = jnp.zeros(w.shape[:3] + (MLA_QK_PAD - MLA_NOPE,), w.dtype)
    wk = jnp.concatenate([k_nope, z], axis=-1).reshape(L, lora, MLA_HEADS * MLA_QK_PAD)
    wv = v.reshape(L, lora, MLA_HEADS * MLA_V)
    return wk.astype(BF16), wv.astype(BF16)


def _prep_rg_gate(w):
    L, nb, bw, _ = w.shape
    per = RG_GROUP // bw
    w = w.reshape(L, nb // per, per, bw, bw)
    eye = jnp.eye(per, dtype=w.dtype)
    tiles = jnp.einsum('lgpjk,pq->lgpjqk', w, eye)
    return tiles.reshape(L, nb // per, RG_GROUP, RG_GROUP).astype(BF16)


def kernel(x, positions, ffn1_norm, ffn1_w_gate_up, ffn1_w_down, mix_norm, w_in, conv_w, conv_b, rg_w_a, rg_b_a, rg_w_x, rg_b_x, rg_lambda, mla_q_norm, mla_w_uq, mla_kv_norm, mla_w_ukv, w_branch_a, w_branch_b, w_branch_c, w_out, ffn2_norm, ffn2_w_gate_up, ffn2_w_down, final_norm):
    B, S, D = x.shape
    depth = w_in.shape[0]
    lora = mla_q_norm.shape[1]
    T = B * S
    tm = min(512, T)
    tm_in = min(256, T)
    ts = min(256, S)
    blk = min(256, S)

    ffn1_wgu = ffn1_w_gate_up.astype(BF16)
    ffn1_wd = ffn1_w_down.astype(BF16)
    ffn2_wgu = ffn2_w_gate_up.astype(BF16)
    ffn2_wd = ffn2_w_down.astype(BF16)
    w_in_r = _prep_w_in(w_in, D, lora)
    wq, wq_sw = _prep_w_uq(mla_w_uq)
    wk, wv = _prep_w_ukv(mla_w_ukv)
    rg_wa = _prep_rg_gate(rg_w_a)
    rg_wx = _prep_rg_gate(rg_w_x)
    wba = w_branch_a.astype(BF16)
    wbb = w_branch_b.astype(BF16)
    wbc = w_branch_c.astype(BF16)
    wo = w_out.astype(BF16)

    cs, sn = _rope_tables(positions, tm)
    xf = x.reshape(T, D)
    row = lambda a, l: a[l][None, :]

    for l in range(depth):
        xf = _ffn(xf, row(ffn1_norm, l), ffn1_wgu[l], ffn1_wd[l], tm=tm)
        rg_x, rg_g, sb_q, sb_k, sb_v, c_q, c_kv, gates, kr = _inproj(
            xf, row(mix_norm, l), w_in_r[l], cs, sn, tm=tm_in, lora=lora)
        y_a = _rglru(rg_x, rg_g, conv_w[l], row(conv_b, l), rg_wa[l], row(rg_b_a, l),
                     rg_wx[l], row(rg_b_x, l), row(rg_lambda, l), batch=B, ts=ts)
        y_b = _sb_attention(sb_q, sb_k, sb_v, batch=B, blk=blk)
        mq, mk, mv = _mla_prep(c_q, c_kv, kr, cs, sn, row(mla_q_norm, l), row(mla_kv_norm, l),
                               wq[l], wq_sw[l], wk[l], wv[l], tm=tm)
        y_c = _mla_attention(mq, mk, mv, batch=B, blk=blk)
        xf = _merge(xf, y_a, y_b, y_c, gates, wba[l], wbb[l], wbc[l], wo[l], tm=tm)
        xf = _ffn(xf, row(ffn2_norm, l), ffn2_wgu[l], ffn2_wd[l],
                  row(final_norm[None, :], 0) if l == depth - 1 else None, tm=tm)

    return xf.reshape(B, S, D)
```

```python
import functools
import math

import jax
import jax.numpy as jnp
from jax import lax
from jax.experimental import pallas as pl
from jax.experimental.pallas import tpu as pltpu

F32 = jnp.float32
BF16 = jnp.bfloat16

NORM_EPS = 1e-6
CHUNK = 64
RG_C = 8.0
CONV_W = 4
SB_HEAD_DIM = 128
MLA_HEADS = 8
MLA_NOPE = 128
MLA_ROPE = 64
MLA_V = 128
MLA_QK_PAD = 256
ROPE_THETA = 10000.0

V7X_VMEM_LIMIT_BYTES = 56 * 1024 * 1024
RG_GROUP = 256
LOG2_E = math.log2(math.e)
SB_DEAD_LOG2 = -105.0 * LOG2_E
SB_HEADS_PER_STEP = 4
SB_BLOCK = 256
ATTN_BLOCK = 512
MLA_HEADS_PER_STEP = 2

_DN_NT = (((1,), (1,)), ((), ()))


def _cparams(sem):
    return pltpu.CompilerParams(dimension_semantics=sem,
                                vmem_limit_bytes=V7X_VMEM_LIMIT_BYTES)


def _const_spec(shape):
    nd = len(shape)
    return pl.BlockSpec(shape, lambda *_: (0,) * nd, pipeline_mode=pl.Buffered(1))


def _rms(x, g):
    return x * lax.rsqrt(jnp.mean(x * x, axis=-1, keepdims=True) + NORM_EPS) * g


def _rope_kernel(pos_ref, inv_ref, cs_ref, sn_ref):
    ang = pos_ref[...] * inv_ref[...]
    lane = lax.broadcasted_iota(jnp.int32, ang.shape, 1)
    half = MLA_ROPE // 2
    c = jnp.cos(ang)
    s = jnp.sin(ang)
    in_rope = (lane >= MLA_NOPE) & (lane < MLA_NOPE + MLA_ROPE)
    cs_ref[...] = jnp.where(lane < MLA_NOPE, 1.0, jnp.where(in_rope, c, 0.0))
    sn_ref[...] = jnp.where(in_rope, jnp.where(lane < MLA_NOPE + half, -s, s), 0.0)


def _rope_tables(positions, tm):
    T = positions.size
    pos = positions.reshape(T, 1).astype(F32)
    inv = ROPE_THETA ** (-jnp.arange(0, MLA_ROPE, 2, dtype=F32) / MLA_ROPE)
    inv_pad = jnp.concatenate([jnp.zeros((MLA_NOPE,), F32), inv, inv,
                               jnp.zeros((MLA_QK_PAD - MLA_NOPE - MLA_ROPE,), F32)])[None, :]
    return pl.pallas_call(
        _rope_kernel,
        grid=(T // tm,),
        in_specs=[pl.BlockSpec((tm, 1), lambda i: (i, 0)),
                  pl.BlockSpec((1, MLA_QK_PAD), lambda i: (0, 0))],
        out_specs=[pl.BlockSpec((tm, MLA_QK_PAD), lambda i: (i, 0))] * 2,
        out_shape=[jax.ShapeDtypeStruct((T, MLA_QK_PAD), F32)] * 2,
        compiler_params=_cparams(("parallel",)),
        name="rope_tables",
    )(pos, inv_pad)


def _ffn_kernel(x_ref, g_ref, wgu_ref, wd_ref, *rest, d_ff, n_chunk, final):
    if final:
        fg_ref, o_ref, a_ref = rest
    else:
        o_ref, a_ref = rest
    x = x_ref[...]
    h = _rms(x, g_ref[...]).astype(BF16)
    tf = d_ff // n_chunk
    for c in range(n_chunk):
        gate = jnp.dot(h, wgu_ref[:, c * tf:(c + 1) * tf], preferred_element_type=F32)
        up = jnp.dot(h, wgu_ref[:, d_ff + c * tf:d_ff + (c + 1) * tf], preferred_element_type=F32)
        a_ref[:, c * tf:(c + 1) * tf] = (gate * jax.nn.sigmoid(gate) * up).astype(BF16)
    y = x + 0.5 * jnp.dot(a_ref[...], wd_ref[...], preferred_element_type=F32)
    if final:
        y = _rms(y, fg_ref[...])
    o_ref[...] = y


def _ffn(x, g, wgu, wd, final_g=None, *, tm):
    T, D = x.shape
    d_ff = wd.shape[0]
    n_chunk = 2 if (d_ff // 2) % 128 == 0 else 1
    final = final_g is not None
    in_specs = [pl.BlockSpec((tm, D), lambda i: (i, 0)),
                _const_spec((1, D)),
                _const_spec((D, 2 * d_ff)),
                _const_spec((d_ff, D))]
    args = [x, g, wgu, wd]
    if final:
        in_specs.append(_const_spec((1, D)))
        args.append(final_g)
    return pl.pallas_call(
        functools.partial(_ffn_kernel, d_ff=d_ff, n_chunk=n_chunk, final=final),
        grid=(T // tm,),
        in_specs=in_specs,
        out_specs=pl.BlockSpec((tm, D), lambda i: (i, 0)),
        out_shape=jax.ShapeDtypeStruct((T, D), F32),
        scratch_shapes=[pltpu.VMEM((tm, d_ff), BF16)],
        compiler_params=_cparams(("parallel",)),
        name="ffn_final" if final else "ffn",
    )(*args)


def _inproj_kernel(x_ref, g_ref, w_ref, cs_ref, sn_ref,
                   rgx_ref, rgg_ref, q_ref, k_ref, v_ref, cq_ref, ckv_ref, gate_ref, kr_ref,
                   *, d, lora, sb_scale):
    h = _rms(x_ref[...], g_ref[...]).astype(BF16)

    def proj(lo, width):
        return jnp.dot(h, w_ref[:, lo:lo + width], preferred_element_type=F32)

    o = 0
    rgx_ref[...] = proj(o, d); o += d
    rgg_ref[...] = proj(o, d); o += d
    q_ref[...] = (proj(o, d) * sb_scale).astype(BF16); o += d
    k_ref[...] = proj(o, d).astype(BF16); o += d
    v_ref[...] = proj(o, d).astype(BF16); o += d
    cq_ref[...] = proj(o, lora); o += lora
    ckv_ref[...] = proj(o, lora); o += lora
    gate_ref[...] = jax.nn.sigmoid(proj(o, 3 * d)); o += 3 * d
    kr = proj(o, MLA_QK_PAD); o += MLA_QK_PAD
    kr_sw = proj(o, MLA_QK_PAD)
    kr_ref[...] = kr * cs_ref[...] + kr_sw * sn_ref[...]


def _inproj(x, g, w, cs, sn, *, tm, lora):
    T, D = x.shape
    n_cols = w.shape[1]
    row = lambda width: pl.BlockSpec((tm, width), lambda i: (i, 0))
    sds = lambda width, dt: jax.ShapeDtypeStruct((T, width), dt)
    return pl.pallas_call(
        functools.partial(_inproj_kernel, d=D, lora=lora, sb_scale=SB_HEAD_DIM ** -0.5 * LOG2_E),
        grid=(T // tm,),
        in_specs=[row(D), _const_spec((1, D)), _const_spec((D, n_cols)),
                  row(MLA_QK_PAD), row(MLA_QK_PAD)],
        out_specs=[row(D), row(D), row(D), row(D), row(D), row(lora), row(lora),
                   row(3 * D), row(MLA_QK_PAD)],
        out_shape=[sds(D, F32), sds(D, F32), sds(D, BF16), sds(D, BF16), sds(D, BF16),
                   sds(lora, F32), sds(lora, F32), sds(3 * D, F32), sds(MLA_QK_PAD, F32)],
        compiler_params=_cparams(("parallel",)),
        name="in_proj",
    )(x, g, w, cs, sn)


def _rglru_kernel(x_ref, gate_ref, cw_ref, cb_ref, wa_ref, ba_ref, wx_ref, bx_ref, lam_ref,
                  o_ref, xbuf, hprev, *, ts, n_group):
    s = pl.program_id(1)
    halo = 8

    @pl.when(s == 0)
    def _():
        xbuf[0:halo, :] = jnp.zeros((halo, xbuf.shape[1]), F32)
        hprev[...] = jnp.zeros(hprev.shape, F32)

    xbuf[halo:halo + ts, :] = x_ref[...]
    row = lax.broadcasted_iota(jnp.int32, (ts, RG_GROUP), 0)

    for gi in range(n_group):
        sl = slice(gi * RG_GROUP, (gi + 1) * RG_GROUP)
        u = cb_ref[:, sl] + cw_ref[CONV_W - 1:CONV_W, sl] * xbuf[halo:halo + ts, sl]
        for back in range(1, CONV_W):
            tap = CONV_W - 1 - back
            u = u + cw_ref[tap:tap + 1, sl] * xbuf[halo - back:halo - back + ts, sl]
        ub = u.astype(BF16)
        r = jax.nn.sigmoid(jnp.dot(ub, wa_ref[gi], preferred_element_type=F32) + ba_ref[:, sl])
        i_gate = jax.nn.sigmoid(jnp.dot(ub, wx_ref[gi], preferred_element_type=F32) + bx_ref[:, sl])
        lam = lam_ref[:, sl]
        neg_softplus = -(jnp.maximum(-lam, 0.0) + jnp.log1p(jnp.exp(-jnp.abs(lam))))
        log_a = (RG_C * neg_softplus) * r
        a = jnp.exp(log_a)
        b = jnp.sqrt(-jnp.tanh(log_a) * (a * a + 1.0)) * (i_gate * u)
        d = 1
        while d < ts:
            m = row >= d
            a_s = pltpu.roll(a, d, 0)
            b_s = pltpu.roll(b, d, 0)
            b = jnp.where(m, a * b_s + b, b)
            a = jnp.where(m, a * a_s, a)
            d *= 2
        hs = a * hprev[:, sl] + b
        hprev[:, sl] = hs[ts - 1:ts, :]
        o_ref[:, sl] = (hs * jax.nn.gelu(gate_ref[:, sl], approximate=True)).astype(o_ref.dtype)

    xbuf[0:halo, :] = x_ref[ts - halo:ts, :]


def _rglru(rg_x, rg_g, cw, cb, wa, ba, wx, bx, lam, *, batch, ts):
    T, C = rg_x.shape
    S = T // batch
    ns = S // ts
    n_group = C // RG_GROUP
    row = pl.BlockSpec((ts, C), lambda b, s: (b * ns + s, 0))
    return pl.pallas_call(
        functools.partial(_rglru_kernel, ts=ts, n_group=n_group),
        grid=(batch, ns),
        in_specs=[row, row,
                  _const_spec((CONV_W, C)), _const_spec((1, C)),
                  _const_spec((n_group, RG_GROUP, RG_GROUP)), _const_spec((1, C)),
                  _const_spec((n_group, RG_GROUP, RG_GROUP)), _const_spec((1, C)),
                  _const_spec((1, C))],
        out_specs=row,
        out_shape=jax.ShapeDtypeStruct((T, C), BF16),
        scratch_shapes=[pltpu.VMEM((ts + 8, C), F32), pltpu.VMEM((1, C), F32)],
        compiler_params=_cparams(("parallel", "arbitrary")),
        name="rglru",
    )(rg_x, rg_g, cw, cb, wa, ba, wx, bx, lam)


def _sb_kernel(q_ref, k_ref, v_ref, tri_ref, o_ref, *, blk, nblk, hp):
    rowi = lax.broadcasted_iota(jnp.int32, (blk, blk), 0)
    coli = lax.broadcasted_iota(jnp.int32, (blk, blk), 1)
    earlier = coli < rowi
    hd = SB_HEAD_DIM

    def load(ref, j, h):
        r0 = j * blk if isinstance(j, int) else pl.multiple_of(j * blk, blk)
        return ref[pl.ds(r0, blk), h * hd:(h + 1) * hd]

    def block_terms(q, j, h, diag):
        z = lax.dot_general(q, load(k_ref, j, h), _DN_NT, preferred_element_type=F32)
        tail = jnp.log(1.0 + jnp.exp2(-jnp.abs(z))) * LOG2_E
        log_beta = jnp.minimum(z, 0.0) - tail
        log_keep = log_beta - z
        if diag:
            log_keep = jnp.where(earlier, log_keep, 0.0)
        hi = log_keep.astype(BF16)
        lo = (log_keep - hi.astype(F32)).astype(BF16)
        tri = tri_ref[...]
        between = (jnp.dot(hi, tri, preferred_element_type=F32)
                   + jnp.dot(lo, tri, preferred_element_type=F32))
        return log_beta + between, jnp.sum(log_keep, axis=1, keepdims=True)

    def pv(w, j, h):
        return jnp.dot(w.astype(BF16), load(v_ref, j, h), preferred_element_type=F32)

    def diag_block(i, h):
        q = load(q_ref, i, h)
        logw, kept = block_terms(q, i, h, True)
        w = jnp.where(earlier, jnp.exp2(logw), 0.0)
        return q, pv(w, i, h), kept

    def any_live(rems):
        top = rems[0]
        for r in rems[1:]:
            top = jnp.maximum(top, r)
        return jnp.max(top) > SB_DEAD_LOG2

    for h in range(hp):
        _, acc, _ = diag_block(0, h)
        o_ref[0:blk, h * hd:(h + 1) * hd] = acc.astype(o_ref.dtype)

    def q_body(i, carry):
        accs, rems = [], []
        for h in range(hp):
            q, acc, kept_d = diag_block(i, h)
            logw, kept_p = block_terms(q, i - 1, h, False)
            accs.append(acc + pv(jnp.exp2(logw + kept_d), i - 1, h))
            rems.append(kept_d + kept_p)

        def cond(c):
            j, live, _, _ = c
            return jnp.logical_and(j >= 0, live)

        def body(c):
            j, _, accs, rems = c
            new_accs, new_rems = [], []
            for h in range(hp):
                logw, kept = block_terms(load(q_ref, i, h), j, h, False)
                new_accs.append(accs[h] + pv(jnp.exp2(logw + rems[h]), j, h))
                new_rems.append(rems[h] + kept)
            return j - 1, any_live(new_rems), tuple(new_accs), tuple(new_rems)

        _, _, accs, _ = lax.while_loop(cond, body, (i - 2, any_live(rems), tuple(accs), tuple(rems)))
        q0 = pl.multiple_of(i * blk, blk)
        for h in range(hp):
            o_ref[pl.ds(q0, blk), h * hd:(h + 1) * hd] = accs[h].astype(o_ref.dtype)
        return carry

    lax.fori_loop(1, nblk, q_body, 0)


def _sb_attention(q, k, v, *, batch, blk, hp):
    T, W = q.shape
    S = T // batch
    heads = W // SB_HEAD_DIM
    tri = (lax.broadcasted_iota(jnp.int32, (blk, blk), 0)
           > lax.broadcasted_iota(jnp.int32, (blk, blk), 1)).astype(BF16)
    spec = pl.BlockSpec((S, hp * SB_HEAD_DIM), lambda b, h: (b, h))
    return pl.pallas_call(
        functools.partial(_sb_kernel, blk=blk, nblk=S // blk, hp=hp),
        grid=(batch, heads // hp),
        in_specs=[spec, spec, spec, _const_spec((blk, blk))],
        out_specs=spec,
        out_shape=jax.ShapeDtypeStruct((T, W), BF16),
        compiler_params=_cparams(("parallel", "parallel")),
        name="sb_attention",
    )(q, k, v, tri)


def _mla_prep_kernel(cq_ref, ckv_ref, kr_ref, cs_ref, sn_ref, qn_ref, kvn_ref,
                     wq_ref, wqs_ref, wk_ref, wv_ref, q_ref, k_ref, v_ref, *, scale):
    cq = _rms(cq_ref[...], qn_ref[...]).astype(BF16)
    ckv = _rms(ckv_ref[...], kvn_ref[...]).astype(BF16)
    cs = cs_ref[...]
    sn = sn_ref[...]
    kr = kr_ref[...]
    v_ref[...] = jnp.dot(ckv, wv_ref[...], preferred_element_type=F32).astype(BF16)
    for h in range(MLA_HEADS):
        sl = slice(h * MLA_QK_PAD, (h + 1) * MLA_QK_PAD)
        qh = jnp.dot(cq, wq_ref[:, sl], preferred_element_type=F32)
        qh_sw = jnp.dot(cq, wqs_ref[:, sl], preferred_element_type=F32)
        q_ref[:, sl] = ((qh * cs + qh_sw * sn) * scale).astype(BF16)
        kh = jnp.dot(ckv, wk_ref[:, sl], preferred_element_type=F32)
        k_ref[:, sl] = (kh + kr).astype(BF16)


def _mla_prep(cq, ckv, kr, cs, sn, qn, kvn, wq, wqs, wk, wv, *, tm):
    T, lora = cq.shape
    qk_w = MLA_HEADS * MLA_QK_PAD
    v_w = MLA_HEADS * MLA_V
    row = lambda width: pl.BlockSpec((tm, width), lambda i: (i, 0))
    return pl.pallas_call(
        functools.partial(_mla_prep_kernel, scale=(MLA_NOPE + MLA_ROPE) ** -0.5 * math.log2(math.e)),
        grid=(T // tm,),
        in_specs=[row(lora), row(lora), row(MLA_QK_PAD), row(MLA_QK_PAD), row(MLA_QK_PAD),
                  _const_spec((1, lora)), _const_spec((1, lora)),
                  _const_spec((lora, qk_w)), _const_spec((lora, qk_w)),
                  _const_spec((lora, qk_w)), _const_spec((lora, v_w))],
        out_specs=[row(qk_w), row(qk_w), row(v_w)],
        out_shape=[jax.ShapeDtypeStruct((T, qk_w), BF16),
                   jax.ShapeDtypeStruct((T, qk_w), BF16),
                   jax.ShapeDtypeStruct((T, v_w), BF16)],
        compiler_params=_cparams(("parallel",)),
        name="mla_prep",
    )(cq, ckv, kr, cs, sn, qn, kvn, wq, wqs, wk, wv)


def _mla_kernel(q_ref, k_ref, v_ref, o_ref, s_scr, *, blk, nblk, hp):
    rowc = lax.broadcasted_iota(jnp.int32, (blk, blk), 0) // CHUNK
    colc = lax.broadcasted_iota(jnp.int32, (blk, blk), 1) // CHUNK
    visible = colc <= rowc
    lanes = MLA_V

    def fold(a, op):
        parts = [a[:, c * lanes:(c + 1) * lanes] for c in range(blk // lanes)]
        while len(parts) > 1:
            parts = [op(parts[n], parts[n + 1]) for n in range(0, len(parts), 2)]
        return parts[0]

    def q_body(i, carry):
        q0 = pl.multiple_of(i * blk, blk)

        def score(h, j):
            k0 = pl.multiple_of(j * blk, blk)
            qk = slice(h * MLA_QK_PAD, (h + 1) * MLA_QK_PAD)
            return lax.dot_general(q_ref[pl.ds(q0, blk), qk], k_ref[pl.ds(k0, blk), qk], _DN_NT,
                                   preferred_element_type=F32)

        def sweep1(j, m_lanes):
            out = []
            for h in range(hp):
                s = score(h, j)
                s_scr[h, j] = s
                out.append(jnp.maximum(m_lanes[h], fold(s, jnp.maximum)))
            return tuple(out)

        m_lanes = []
        for h in range(hp):
            s_diag = jnp.where(visible, score(h, i), -jnp.inf)
            s_scr[h, i] = s_diag
            m_lanes.append(fold(s_diag, jnp.maximum))
        m_lanes = lax.fori_loop(0, i, sweep1, tuple(m_lanes))
        ms = [jnp.max(m_lane, axis=1, keepdims=True) for m_lane in m_lanes]

        def sweep2(j, c):
            k0 = pl.multiple_of(j * blk, blk)
            out = []
            for h in range(hp):
                l_lane, acc = c[h]
                p = jnp.exp2(s_scr[h, j] - ms[h])
                pv = jnp.dot(p.astype(BF16), v_ref[pl.ds(k0, blk), h * MLA_V:(h + 1) * MLA_V],
                             preferred_element_type=F32)
                out.append((l_lane + fold(p, jnp.add), acc + pv))
            return tuple(out)

        zero = (jnp.zeros((blk, lanes), F32), jnp.zeros((blk, MLA_V), F32))
        c = lax.fori_loop(0, i + 1, sweep2, (zero,) * hp)
        for h in range(hp):
            l_lane, acc = c[h]
            l = jnp.sum(l_lane, axis=1, keepdims=True)
            o_ref[pl.ds(q0, blk), h * MLA_V:(h + 1) * MLA_V] = (acc / l).astype(o_ref.dtype)
        return carry

    lax.fori_loop(0, nblk, q_body, 0)


def _mla_attention(q, k, v, *, batch, blk, hp):
    T = q.shape[0]
    S = T // batch
    qk_spec = pl.BlockSpec((S, hp * MLA_QK_PAD), lambda b, h: (b, h))
    v_spec = pl.BlockSpec((S, hp * MLA_V), lambda b, h: (b, h))
    return pl.pallas_call(
        functools.partial(_mla_kernel, blk=blk, nblk=S // blk, hp=hp),
        grid=(batch, MLA_HEADS // hp),
        in_specs=[qk_spec, qk_spec, v_spec],
        out_specs=v_spec,
        out_shape=jax.ShapeDtypeStruct((T, MLA_HEADS * MLA_V), BF16),
        scratch_shapes=[pltpu.VMEM((hp, S // blk, blk, blk), F32)],
        compiler_params=_cparams(("parallel", "parallel")),
        name="mla_attention",
    )(q, k, v)


def _merge_kernel(x_ref, ya_ref, yb_ref, yc_ref, gate_ref, wa_ref, wb_ref, wc_ref, wo_ref, o_ref, *, d):
    merged = gate_ref[:, 0:d] * jnp.dot(ya_ref[...], wa_ref[...], preferred_element_type=F32)
    merged = merged + gate_ref[:, d:2 * d] * jnp.dot(yb_ref[...], wb_ref[...], preferred_element_type=F32)
    merged = merged + gate_ref[:, 2 * d:3 * d] * jnp.dot(yc_ref[...], wc_ref[...], preferred_element_type=F32)
    o_ref[...] = x_ref[...] + jnp.dot(merged.astype(BF16), wo_ref[...], preferred_element_type=F32)


def _merge(x, ya, yb, yc, gates, wa, wb, wc, wo, *, tm):
    T, D = x.shape
    row = lambda width: pl.BlockSpec((tm, width), lambda i: (i, 0))
    wspec = _const_spec((D, D))
    return pl.pallas_call(
        functools.partial(_merge_kernel, d=D),
        grid=(T // tm,),
        in_specs=[row(D), row(D), row(D), row(D), row(3 * D), wspec, wspec, wspec, wspec],
        out_specs=row(D),
        out_shape=jax.ShapeDtypeStruct((T, D), F32),
        compiler_params=_cparams(("parallel",)),
        name="merge",
    )(x, ya, yb, yc, gates, wa, wb, wc, wo)


def _prep_w_in(w_in, d, lora):
    kr_lo = 5 * d + 2 * lora
    kr_hi = kr_lo + MLA_ROPE
    half = MLA_ROPE // 2
    kr = w_in[:, :, kr_lo:kr_hi]
    lead = w_in.shape[:2]
    z_nope = jnp.zeros(lead + (MLA_NOPE,), w_in.dtype)
    z_tail = jnp.zeros(lead + (MLA_QK_PAD - MLA_NOPE - MLA_ROPE,), w_in.dtype)
    kr_pad = jnp.concatenate([z_nope, kr[..., :half], kr[..., half:], z_tail], axis=-1)
    kr_swp = jnp.concatenate([z_nope, kr[..., half:], kr[..., :half], z_tail], axis=-1)
    return jnp.concatenate([w_in[:, :, :kr_lo], w_in[:, :, kr_hi:], kr_pad, kr_swp], axis=-1).astype(BF16)


def _prep_w_uq(w_uq):
    L, lora, _ = w_uq.shape
    half = MLA_ROPE // 2
    w = w_uq.reshape(L, lora, MLA_HEADS, MLA_NOPE + MLA_ROPE)
    nope, r1, r2 = w[..., :MLA_NOPE], w[..., MLA_NOPE:MLA_NOPE + half], w[..., MLA_NOPE + half:]
    z_tail = jnp.zeros(w.shape[:3] + (MLA_QK_PAD - MLA_NOPE - MLA_ROPE,), w.dtype)
    plain = jnp.concatenate([nope, r1, r2, z_tail], axis=-1)
    swapped = jnp.concatenate([jnp.zeros_like(nope), r2, r1, z_tail], axis=-1)
    shape = (L, lora, MLA_HEADS * MLA_QK_PAD)
    return plain.reshape(shape).astype(BF16), swapped.reshape(shape).astype(BF16)


def _prep_w_ukv(w_ukv):
    L, lora, _ = w_ukv.shape
    w = w_ukv.reshape(L, lora, MLA_HEADS, MLA_NOPE + MLA_V)
    k_nope, v = w[..., :MLA_NOPE], w[..., MLA_NOPE:]
    z = jnp.zeros(w.shape[:3] + (MLA_QK_PAD - MLA_NOPE,), w.dtype)
    wk = jnp.concatenate([k_nope, z], axis=-1).reshape(L, lora, MLA_HEADS * MLA_QK_PAD)
    wv = v.reshape(L, lora, MLA_HEADS * MLA_V)
    return wk.astype(BF16), wv.astype(BF16)


def _prep_rg_gate(w):
    L, nb, bw, _ = w.shape
    per = RG_GROUP // bw
    w = w.reshape(L, nb // per, per, bw, bw)
    eye = jnp.eye(per, dtype=w.dtype)
    tiles = jnp.einsum('lgpjk,pq->lgpjqk', w, eye)
    return tiles.reshape(L, nb // per, RG_GROUP, RG_GROUP).astype(BF16)


def kernel(x, positions, ffn1_norm, ffn1_w_gate_up, ffn1_w_down, mix_norm, w_in, conv_w, conv_b, rg_w_a, rg_b_a, rg_w_x, rg_b_x, rg_lambda, mla_q_norm, mla_w_uq, mla_kv_norm, mla_w_ukv, w_branch_a, w_branch_b, w_branch_c, w_out, ffn2_norm, ffn2_w_gate_up, ffn2_w_down, final_norm):
    B, S, D = x.shape
    depth = w_in.shape[0]
    lora = mla_q_norm.shape[1]
    T = B * S
    tm = min(512, T)
    tm_in = min(256, T)
    ts = min(256, S)
    blk = min(ATTN_BLOCK, S)
    sb_blk = min(SB_BLOCK, S)

    ffn1_wgu = ffn1_w_gate_up.astype(BF16)
    ffn1_wd = ffn1_w_down.astype(BF16)
    ffn2_wgu = ffn2_w_gate_up.astype(BF16)
    ffn2_wd = ffn2_w_down.astype(BF16)
    w_in_r = _prep_w_in(w_in, D, lora)
    wq, wq_sw = _prep_w_uq(mla_w_uq)
    wk, wv = _prep_w_ukv(mla_w_ukv)
    rg_wa = _prep_rg_gate(rg_w_a)
    rg_wx = _prep_rg_gate(rg_w_x)
    wba = w_branch_a.astype(BF16)
    wbb = w_branch_b.astype(BF16)
    wbc = w_branch_c.astype(BF16)
    wo = w_out.astype(BF16)

    cs, sn = _rope_tables(positions, tm)
    xf = x.reshape(T, D)
    row = lambda a, l: a[l][None, :]

    for l in range(depth):
        xf = _ffn(xf, row(ffn1_norm, l), ffn1_wgu[l], ffn1_wd[l], tm=tm)
        rg_x, rg_g, sb_q, sb_k, sb_v, c_q, c_kv, gates, kr = _inproj(
            xf, row(mix_norm, l), w_in_r[l], cs, sn, tm=tm_in, lora=lora)
        y_a = _rglru(rg_x, rg_g, conv_w[l], row(conv_b, l), rg_wa[l], row(rg_b_a, l),
                     rg_wx[l], row(rg_b_x, l), row(rg_lambda, l), batch=B, ts=ts)
        y_b = _sb_attention(sb_q, sb_k, sb_v, batch=B, blk=sb_blk, hp=SB_HEADS_PER_STEP)
        mq, mk, mv = _mla_prep(c_q, c_kv, kr, cs, sn, row(mla_q_norm, l), row(mla_kv_norm, l),
                               wq[l], wq_sw[l], wk[l], wv[l], tm=tm)
        y_c = _mla_attention(mq, mk, mv, batch=B, blk=blk, hp=MLA_HEADS_PER_STEP)
        xf = _merge(xf, y_a, y_b, y_c, gates, wba[l], wbb[l], wbc[l], wo[l], tm=tm)
        xf = _ffn(xf, row(ffn2_norm, l), ffn2_wgu[l], ffn2_wd[l],
                  row(final_norm[None, :], 0) if l == depth - 1 else None, tm=tm)

    return xf.reshape(B, S, D)
```

```python
import functools
import math

import jax
import jax.numpy as jnp
from jax import lax
from jax.experimental import pallas as pl
from jax.experimental.pallas import tpu as pltpu

F32 = jnp.float32
BF16 = jnp.bfloat16

NORM_EPS = 1e-6
CHUNK = 64
RG_C = 8.0
CONV_W = 4
SB_HEAD_DIM = 128
MLA_HEADS = 8
MLA_NOPE = 128
MLA_ROPE = 64
MLA_V = 128
MLA_QK_PAD = 256
ROPE_THETA = 10000.0

V7X_VMEM_LIMIT_BYTES = 56 * 1024 * 1024
RG_GROUP = 256
SUBLANES = 8
LOG2_E = math.log2(math.e)
SB_DEAD_LOG2 = -105.0 * LOG2_E
SB_HEADS_PER_STEP = 4
SB_BLOCK = 256
ATTN_BLOCK = 512
MLA_HEADS_PER_STEP = 2

_DN_NT = (((1,), (1,)), ((), ()))


def _cparams(sem):
    return pltpu.CompilerParams(dimension_semantics=sem,
                                vmem_limit_bytes=V7X_VMEM_LIMIT_BYTES)


def _const_spec(shape):
    nd = len(shape)
    return pl.BlockSpec(shape, lambda *_: (0,) * nd, pipeline_mode=pl.Buffered(1))


def _layer_spec(shape, l):
    nd = len(shape)
    return pl.BlockSpec((None,) + tuple(shape), lambda *_: (l,) + (0,) * nd,
                        pipeline_mode=pl.Buffered(1))


def _rms(x, g):
    return x * lax.rsqrt(jnp.mean(x * x, axis=-1, keepdims=True) + NORM_EPS) * g


def _rope_kernel(pos_ref, inv_ref, cs_ref, sn_ref):
    ang = pos_ref[...] * inv_ref[...]
    lane = lax.broadcasted_iota(jnp.int32, ang.shape, 1)
    half = MLA_ROPE // 2
    c = jnp.cos(ang)
    s = jnp.sin(ang)
    in_rope = (lane >= MLA_NOPE) & (lane < MLA_NOPE + MLA_ROPE)
    cs_ref[...] = jnp.where(lane < MLA_NOPE, 1.0, jnp.where(in_rope, c, 0.0))
    sn_ref[...] = jnp.where(in_rope, jnp.where(lane < MLA_NOPE + half, -s, s), 0.0)


def _rope_tables(positions, tm):
    T = positions.size
    pos = positions.reshape(T, 1).astype(F32)
    inv = ROPE_THETA ** (-jnp.arange(0, MLA_ROPE, 2, dtype=F32) / MLA_ROPE)
    inv_pad = jnp.concatenate([jnp.zeros((MLA_NOPE,), F32), inv, inv,
                               jnp.zeros((MLA_QK_PAD - MLA_NOPE - MLA_ROPE,), F32)])[None, :]
    return pl.pallas_call(
        _rope_kernel,
        grid=(T // tm,),
        in_specs=[pl.BlockSpec((tm, 1), lambda i: (i, 0)),
                  pl.BlockSpec((1, MLA_QK_PAD), lambda i: (0, 0))],
        out_specs=[pl.BlockSpec((tm, MLA_QK_PAD), lambda i: (i, 0))] * 2,
        out_shape=[jax.ShapeDtypeStruct((T, MLA_QK_PAD), F32)] * 2,
        compiler_params=_cparams(("parallel",)),
        name="rope_tables",
    )(pos, inv_pad)


def _ffn_kernel(x_ref, g_ref, wgu_ref, wd_ref, *rest, d_ff, n_chunk, final):
    if final:
        fg_ref, o_ref, a_ref = rest
    else:
        o_ref, a_ref = rest
    x = x_ref[...]
    h = _rms(x, g_ref[...]).astype(BF16)
    tf = d_ff // n_chunk
    for c in range(n_chunk):
        gate = jnp.dot(h, wgu_ref[:, c * tf:(c + 1) * tf], preferred_element_type=F32)
        up = jnp.dot(h, wgu_ref[:, d_ff + c * tf:d_ff + (c + 1) * tf], preferred_element_type=F32)
        a_ref[:, c * tf:(c + 1) * tf] = (gate * jax.nn.sigmoid(gate) * up).astype(BF16)
    y = x + 0.5 * jnp.dot(a_ref[...], wd_ref[...], preferred_element_type=F32)
    if final:
        y = _rms(y, fg_ref[...])
    o_ref[...] = y


def _ffn(x, g, wgu, wd, l, final_g=None, *, tm):
    T, D = x.shape
    d_ff = wd.shape[1]
    n_chunk = 2 if (d_ff // 2) % 128 == 0 else 1
    final = final_g is not None
    in_specs = [pl.BlockSpec((tm, D), lambda i: (i, 0)),
                _layer_spec((1, D), l),
                _layer_spec((D, 2 * d_ff), l),
                _layer_spec((d_ff, D), l)]
    args = [x, g, wgu, wd]
    if final:
        in_specs.append(_const_spec((1, D)))
        args.append(final_g)
    return pl.pallas_call(
        functools.partial(_ffn_kernel, d_ff=d_ff, n_chunk=n_chunk, final=final),
        grid=(T // tm,),
        in_specs=in_specs,
        out_specs=pl.BlockSpec((tm, D), lambda i: (i, 0)),
        out_shape=jax.ShapeDtypeStruct((T, D), F32),
        scratch_shapes=[pltpu.VMEM((tm, d_ff), BF16)],
        compiler_params=_cparams(("parallel",)),
        name="ffn_final" if final else "ffn",
    )(*args)


def _inproj_kernel(x_ref, g_ref, w_ref, cs_ref, sn_ref,
                   rgx_ref, rgg_ref, q_ref, k_ref, v_ref, cq_ref, ckv_ref, gate_ref, kr_ref,
                   *, d, lora, sb_scale):
    h = _rms(x_ref[...], g_ref[...]).astype(BF16)

    def proj(lo, width):
        return jnp.dot(h, w_ref[:, lo:lo + width], preferred_element_type=F32)

    o = 0
    rgx_ref[...] = proj(o, d); o += d
    rgg_ref[...] = proj(o, d); o += d
    q_ref[...] = (proj(o, d) * sb_scale).astype(BF16); o += d
    k_ref[...] = proj(o, d).astype(BF16); o += d
    v_ref[...] = proj(o, d).astype(BF16); o += d
    cq_ref[...] = proj(o, lora); o += lora
    ckv_ref[...] = proj(o, lora); o += lora
    gate_ref[...] = jax.nn.sigmoid(proj(o, 3 * d)); o += 3 * d
    kr = proj(o, MLA_QK_PAD); o += MLA_QK_PAD
    kr_sw = proj(o, MLA_QK_PAD)
    kr_ref[...] = kr * cs_ref[...] + kr_sw * sn_ref[...]


def _inproj(x, g, w, cs, sn, l, *, tm, lora):
    T, D = x.shape
    n_cols = w.shape[2]
    row = lambda width: pl.BlockSpec((tm, width), lambda i: (i, 0))
    sds = lambda width, dt: jax.ShapeDtypeStruct((T, width), dt)
    return pl.pallas_call(
        functools.partial(_inproj_kernel, d=D, lora=lora, sb_scale=SB_HEAD_DIM ** -0.5 * LOG2_E),
        grid=(T // tm,),
        in_specs=[row(D), _layer_spec((1, D), l), _layer_spec((D, n_cols), l),
                  row(MLA_QK_PAD), row(MLA_QK_PAD)],
        out_specs=[row(D), row(D), row(D), row(D), row(D), row(lora), row(lora),
                   row(3 * D), row(MLA_QK_PAD)],
        out_shape=[sds(D, F32), sds(D, F32), sds(D, BF16), sds(D, BF16), sds(D, BF16),
                   sds(lora, F32), sds(lora, F32), sds(3 * D, F32), sds(MLA_QK_PAD, F32)],
        compiler_params=_cparams(("parallel",)),
        name="in_proj",
    )(x, g, w, cs, sn)


def _rglru_kernel(x_ref, gate_ref, cw_ref, cb_ref, wa_ref, ba_ref, wx_ref, bx_ref, lam_ref,
                  o_ref, xbuf, hprev, *, ts, n_group):
    s = pl.program_id(1)
    halo = SUBLANES

    @pl.when(s == 0)
    def _():
        xbuf[0:halo, :] = jnp.zeros((halo, xbuf.shape[1]), F32)
        hprev[...] = jnp.zeros(hprev.shape, F32)

    xbuf[halo:halo + ts, :] = x_ref[...]
    n_sub = ts // SUBLANES
    sub = lax.broadcasted_iota(jnp.int32, (n_sub, SUBLANES, RG_GROUP), 1)

    for gi in range(n_group):
        sl = slice(gi * RG_GROUP, (gi + 1) * RG_GROUP)
        u = cb_ref[:, sl] + cw_ref[CONV_W - 1:CONV_W, sl] * xbuf[halo:halo + ts, sl]
        for back in range(1, CONV_W):
            tap = CONV_W - 1 - back
            u = u + cw_ref[tap:tap + 1, sl] * xbuf[halo - back:halo - back + ts, sl]
        ub = u.astype(BF16)
        r = jax.nn.sigmoid(jnp.dot(ub, wa_ref[gi], preferred_element_type=F32) + ba_ref[:, sl])
        i_gate = jax.nn.sigmoid(jnp.dot(ub, wx_ref[gi], preferred_element_type=F32) + bx_ref[:, sl])
        lam = lam_ref[:, sl]
        neg_softplus = -(jnp.maximum(-lam, 0.0) + jnp.log1p(jnp.exp(-jnp.abs(lam))))
        log_a = (RG_C * neg_softplus) * r
        a = jnp.exp(log_a)
        b = jnp.sqrt(-jnp.tanh(log_a) * (a * a + 1.0)) * (i_gate * u)
        a = a.reshape(n_sub, SUBLANES, RG_GROUP)
        b = b.reshape(n_sub, SUBLANES, RG_GROUP)
        d = 1
        while d < SUBLANES:
            m = sub >= d
            a_s = pltpu.roll(a, d, 1)
            b_s = pltpu.roll(b, d, 1)
            b = jnp.where(m, a * b_s + b, b)
            a = jnp.where(m, a * a_s, a)
            d *= 2
        state = hprev[:, sl]
        groups = []
        for g in range(n_sub):
            hg = a[g] * state + b[g]
            state = hg[SUBLANES - 1:SUBLANES, :]
            groups.append(hg)
        hs = jnp.concatenate(groups, axis=0)
        hprev[:, sl] = state
        o_ref[:, sl] = (hs * jax.nn.gelu(gate_ref[:, sl], approximate=True)).astype(o_ref.dtype)

    xbuf[0:halo, :] = x_ref[ts - halo:ts, :]


def _rglru(rg_x, rg_g, cw, cb, wa, ba, wx, bx, lam, l, *, batch, ts):
    T, C = rg_x.shape
    S = T // batch
    ns = S // ts
    n_group = C // RG_GROUP
    row = pl.BlockSpec((ts, C), lambda b, s: (b * ns + s, 0))
    return pl.pallas_call(
        functools.partial(_rglru_kernel, ts=ts, n_group=n_group),
        grid=(batch, ns),
        in_specs=[row, row,
                  _layer_spec((CONV_W, C), l), _layer_spec((1, C), l),
                  _layer_spec((n_group, RG_GROUP, RG_GROUP), l), _layer_spec((1, C), l),
                  _layer_spec((n_group, RG_GROUP, RG_GROUP), l), _layer_spec((1, C), l),
                  _layer_spec((1, C), l)],
        out_specs=row,
        out_shape=jax.ShapeDtypeStruct((T, C), BF16),
        scratch_shapes=[pltpu.VMEM((ts + SUBLANES, C), F32), pltpu.VMEM((1, C), F32)],
        compiler_params=_cparams(("parallel", "arbitrary")),
        name="rglru",
    )(rg_x, rg_g, cw, cb, wa, ba, wx, bx, lam)


def _sb_kernel(q_ref, k_ref, v_ref, tri_ref, o_ref, *, blk, nblk, hp):
    rowi = lax.broadcasted_iota(jnp.int32, (blk, blk), 0)
    coli = lax.broadcasted_iota(jnp.int32, (blk, blk), 1)
    earlier = coli < rowi
    hd = SB_HEAD_DIM

    def load(ref, j, h):
        r0 = j * blk if isinstance(j, int) else pl.multiple_of(j * blk, blk)
        return ref[pl.ds(r0, blk), h * hd:(h + 1) * hd]

    def block_terms(q, j, h, diag):
        z = lax.dot_general(q, load(k_ref, j, h), _DN_NT, preferred_element_type=F32)
        tail = jnp.log(1.0 + jnp.exp2(-jnp.abs(z))) * LOG2_E
        log_beta = jnp.minimum(z, 0.0) - tail
        log_keep = log_beta - z
        if diag:
            log_keep = jnp.where(earlier, log_keep, 0.0)
        hi = log_keep.astype(BF16)
        lo = (log_keep - hi.astype(F32)).astype(BF16)
        tri = tri_ref[...]
        between = (jnp.dot(hi, tri, preferred_element_type=F32)
                   + jnp.dot(lo, tri, preferred_element_type=F32))
        return log_beta + between, jnp.sum(log_keep, axis=1, keepdims=True)

    def pv(w, j, h):
        return jnp.dot(w.astype(BF16), load(v_ref, j, h), preferred_element_type=F32)

    def diag_block(i, h):
        q = load(q_ref, i, h)
        logw, kept = block_terms(q, i, h, True)
        w = jnp.where(earlier, jnp.exp2(logw), 0.0)
        return q, pv(w, i, h), kept

    def any_live(rems):
        top = rems[0]
        for r in rems[1:]:
            top = jnp.maximum(top, r)
        return jnp.max(top) > SB_DEAD_LOG2

    for h in range(hp):
        _, acc, _ = diag_block(0, h)
        o_ref[0:blk, h * hd:(h + 1) * hd] = acc.astype(o_ref.dtype)

    def q_body(i, carry):
        accs, rems = [], []
        for h in range(hp):
            q, acc, kept_d = diag_block(i, h)
            logw, kept_p = block_terms(q, i - 1, h, False)
            accs.append(acc + pv(jnp.exp2(logw + kept_d), i - 1, h))
            rems.append(kept_d + kept_p)

        def cond(c):
            j, live, _, _ = c
            return jnp.logical_and(j >= 0, live)

        def body(c):
            j, _, accs, rems = c
            new_accs, new_rems = [], []
            for h in range(hp):
                logw, kept = block_terms(load(q_ref, i, h), j, h, False)
                new_accs.append(accs[h] + pv(jnp.exp2(logw + rems[h]), j, h))
                new_rems.append(rems[h] + kept)
            return j - 1, any_live(new_rems), tuple(new_accs), tuple(new_rems)

        _, _, accs, _ = lax.while_loop(cond, body, (i - 2, any_live(rems), tuple(accs), tuple(rems)))
        q0 = pl.multiple_of(i * blk, blk)
        for h in range(hp):
            o_ref[pl.ds(q0, blk), h * hd:(h + 1) * hd] = accs[h].astype(o_ref.dtype)
        return carry

    lax.fori_loop(1, nblk, q_body, 0)


def _sb_attention(q, k, v, *, batch, blk, hp):
    T, W = q.shape
    S = T // batch
    heads = W // SB_HEAD_DIM
    tri = (lax.broadcasted_iota(jnp.int32, (blk, blk), 0)
           > lax.broadcasted_iota(jnp.int32, (blk, blk), 1)).astype(BF16)
    spec = pl.BlockSpec((S, hp * SB_HEAD_DIM), lambda b, h: (b, h))
    return pl.pallas_call(
        functools.partial(_sb_kernel, blk=blk, nblk=S // blk, hp=hp),
        grid=(batch, heads // hp),
        in_specs=[spec, spec, spec, _const_spec((blk, blk))],
        out_specs=spec,
        out_shape=jax.ShapeDtypeStruct((T, W), BF16),
        compiler_params=_cparams(("parallel", "parallel")),
        name="sb_attention",
    )(q, k, v, tri)


def _mla_prep_kernel(cq_ref, ckv_ref, kr_ref, cs_ref, sn_ref, qn_ref, kvn_ref,
                     wq_ref, wqs_ref, wk_ref, wv_ref, q_ref, k_ref, v_ref, *, scale):
    cq = _rms(cq_ref[...], qn_ref[...]).astype(BF16)
    ckv = _rms(ckv_ref[...], kvn_ref[...]).astype(BF16)
    cs = cs_ref[...]
    sn = sn_ref[...]
    kr = kr_ref[...]
    v_ref[...] = jnp.dot(ckv, wv_ref[...], preferred_element_type=F32).astype(BF16)
    for h in range(MLA_HEADS):
        sl = slice(h * MLA_QK_PAD, (h + 1) * MLA_QK_PAD)
        qh = jnp.dot(cq, wq_ref[:, sl], preferred_element_type=F32)
        qh_sw = jnp.dot(cq, wqs_ref[:, sl], preferred_element_type=F32)
        q_ref[:, sl] = ((qh * cs + qh_sw * sn) * scale).astype(BF16)
        kh = jnp.dot(ckv, wk_ref[:, sl], preferred_element_type=F32)
        k_ref[:, sl] = (kh + kr).astype(BF16)


def _mla_prep(cq, ckv, kr, cs, sn, qn, kvn, wq, wqs, wk, wv, l, *, tm):
    T, lora = cq.shape
    qk_w = MLA_HEADS * MLA_QK_PAD
    v_w = MLA_HEADS * MLA_V
    row = lambda width: pl.BlockSpec((tm, width), lambda i: (i, 0))
    return pl.pallas_call(
        functools.partial(_mla_prep_kernel, scale=(MLA_NOPE + MLA_ROPE) ** -0.5 * math.log2(math.e)),
        grid=(T // tm,),
        in_specs=[row(lora), row(lora), row(MLA_QK_PAD), row(MLA_QK_PAD), row(MLA_QK_PAD),
                  _layer_spec((1, lora), l), _layer_spec((1, lora), l),
                  _layer_spec((lora, qk_w), l), _layer_spec((lora, qk_w), l),
                  _layer_spec((lora, qk_w), l), _layer_spec((lora, v_w), l)],
        out_specs=[row(qk_w), row(qk_w), row(v_w)],
        out_shape=[jax.ShapeDtypeStruct((T, qk_w), BF16),
                   jax.ShapeDtypeStruct((T, qk_w), BF16),
                   jax.ShapeDtypeStruct((T, v_w), BF16)],
        compiler_params=_cparams(("parallel",)),
        name="mla_prep",
    )(cq, ckv, kr, cs, sn, qn, kvn, wq, wqs, wk, wv)


def _mla_kernel(q_ref, k_ref, v_ref, o_ref, s_scr, *, blk, nblk, hp):
    rowc = lax.broadcasted_iota(jnp.int32, (blk, blk), 0) // CHUNK
    colc = lax.broadcasted_iota(jnp.int32, (blk, blk), 1) // CHUNK
    visible = colc <= rowc
    lanes = MLA_V

    def fold(a, op):
        parts = [a[:, c * lanes:(c + 1) * lanes] for c in range(a.shape[1] // lanes)]
        while len(parts) > 1:
            parts = [op(parts[n], parts[n + 1]) for n in range(0, len(parts), 2)]
        return parts[0]

    def q_body(i, carry):
        q0 = pl.multiple_of(i * blk, blk)

        def score(h, j, n):
            k0 = pl.multiple_of(j * blk, blk)
            qk = slice(h * MLA_QK_PAD, (h + 1) * MLA_QK_PAD)
            return lax.dot_general(q_ref[pl.ds(q0, blk), qk], k_ref[pl.ds(k0, n * blk), qk], _DN_NT,
                                   preferred_element_type=F32)

        def sweep1(j, n, m_lanes):
            out = []
            for h in range(hp):
                s = score(h, j, n)
                for c in range(n):
                    s_scr[h, j + c] = s[:, c * blk:(c + 1) * blk]
                out.append(jnp.maximum(m_lanes[h], fold(s, jnp.maximum)))
            return tuple(out)

        m_lanes = []
        for h in range(hp):
            s_diag = jnp.where(visible, score(h, i, 1), -jnp.inf)
            s_scr[h, i] = s_diag
            m_lanes.append(fold(s_diag, jnp.maximum))
        m_lanes = lax.fori_loop(0, lax.shift_right_logical(i, 1),
                                lambda t, m: sweep1(2 * t, 2, m), tuple(m_lanes))
        m_lanes = lax.cond((i & 1) == 1, lambda m: sweep1(i - 1, 1, m), lambda m: m, m_lanes)
        ms = [jnp.max(m_lane, axis=1, keepdims=True) for m_lane in m_lanes]

        def sweep2(j, n, c):
            k0 = pl.multiple_of(j * blk, blk)
            out = []
            for h in range(hp):
                l_lane, acc = c[h]
                s = jnp.concatenate([s_scr[h, j + b] for b in range(n)], axis=1) if n > 1 else s_scr[h, j]
                p = jnp.exp2(s - ms[h])
                pv = jnp.dot(p.astype(BF16), v_ref[pl.ds(k0, n * blk), h * MLA_V:(h + 1) * MLA_V],
                             preferred_element_type=F32)
                out.append((l_lane + fold(p, jnp.add), acc + pv))
            return tuple(out)

        zero = (jnp.zeros((blk, lanes), F32), jnp.zeros((blk, MLA_V), F32))
        n_blocks = i + 1
        c = lax.fori_loop(0, lax.shift_right_logical(n_blocks, 1),
                          lambda t, c: sweep2(2 * t, 2, c), (zero,) * hp)
        c = lax.cond((n_blocks & 1) == 1, lambda c: sweep2(i, 1, c), lambda c: c, c)
        for h in range(hp):
            l_lane, acc = c[h]
            l = jnp.sum(l_lane, axis=1, keepdims=True)
            o_ref[pl.ds(q0, blk), h * MLA_V:(h + 1) * MLA_V] = (acc / l).astype(o_ref.dtype)
        return carry

    lax.fori_loop(0, nblk, q_body, 0)


def _mla_attention(q, k, v, *, batch, blk, hp):
    T = q.shape[0]
    S = T // batch
    qk_spec = pl.BlockSpec((S, hp * MLA_QK_PAD), lambda b, h: (b, h))
    v_spec = pl.BlockSpec((S, hp * MLA_V), lambda b, h: (b, h))
    return pl.pallas_call(
        functools.partial(_mla_kernel, blk=blk, nblk=S // blk, hp=hp),
        grid=(batch, MLA_HEADS // hp),
        in_specs=[qk_spec, qk_spec, v_spec],
        out_specs=v_spec,
        out_shape=jax.ShapeDtypeStruct((T, MLA_HEADS * MLA_V), BF16),
        scratch_shapes=[pltpu.VMEM((hp, S // blk, blk, blk), F32)],
        compiler_params=_cparams(("parallel", "parallel")),
        name="mla_attention",
    )(q, k, v)


def _merge_kernel(x_ref, ya_ref, yb_ref, yc_ref, gate_ref, wa_ref, wb_ref, wc_ref, wo_ref, o_ref, *, d):
    merged = gate_ref[:, 0:d] * jnp.dot(ya_ref[...], wa_ref[...], preferred_element_type=F32)
    merged = merged + gate_ref[:, d:2 * d] * jnp.dot(yb_ref[...], wb_ref[...], preferred_element_type=F32)
    merged = merged + gate_ref[:, 2 * d:3 * d] * jnp.dot(yc_ref[...], wc_ref[...], preferred_element_type=F32)
    o_ref[...] = x_ref[...] + jnp.dot(merged.astype(BF16), wo_ref[...], preferred_element_type=F32)


def _merge(x, ya, yb, yc, gates, wa, wb, wc, wo, l, *, tm):
    T, D = x.shape
    row = lambda width: pl.BlockSpec((tm, width), lambda i: (i, 0))
    wspec = _layer_spec((D, D), l)
    return pl.pallas_call(
        functools.partial(_merge_kernel, d=D),
        grid=(T // tm,),
        in_specs=[row(D), row(D), row(D), row(D), row(3 * D), wspec, wspec, wspec, wspec],
        out_specs=row(D),
        out_shape=jax.ShapeDtypeStruct((T, D), F32),
        compiler_params=_cparams(("parallel",)),
        name="merge",
    )(x, ya, yb, yc, gates, wa, wb, wc, wo)


def _prep_w_in(w_in, d, lora):
    kr_lo = 5 * d + 2 * lora
    kr_hi = kr_lo + MLA_ROPE
    half = MLA_ROPE // 2
    kr = w_in[:, :, kr_lo:kr_hi]
    lead = w_in.shape[:2]
    z_nope = jnp.zeros(lead + (MLA_NOPE,), w_in.dtype)
    z_tail = jnp.zeros(lead + (MLA_QK_PAD - MLA_NOPE - MLA_ROPE,), w_in.dtype)
    kr_pad = jnp.concatenate([z_nope, kr[..., :half], kr[..., half:], z_tail], axis=-1)
    kr_swp = jnp.concatenate([z_nope, kr[..., half:], kr[..., :half], z_tail], axis=-1)
    return jnp.concatenate([w_in[:, :, :kr_lo], w_in[:, :, kr_hi:], kr_pad, kr_swp], axis=-1).astype(BF16)


def _prep_w_uq(w_uq):
    L, lora, _ = w_uq.shape
    half = MLA_ROPE // 2
    w = w_uq.reshape(L, lora, MLA_HEADS, MLA_NOPE + MLA_ROPE)
    nope, r1, r2 = w[..., :MLA_NOPE], w[..., MLA_NOPE:MLA_NOPE + half], w[..., MLA_NOPE + half:]
    z_tail = jnp.zeros(w.shape[:3] + (MLA_QK_PAD - MLA_NOPE - MLA_ROPE,), w.dtype)
    plain = jnp.concatenate([nope, r1, r2, z_tail], axis=-1)
    swapped = jnp.concatenate([jnp.zeros_like(nope), r2, r1, z_tail], axis=-1)
    shape = (L, lora, MLA_HEADS * MLA_QK_PAD)
    return plain.reshape(shape).astype(BF16), swapped.reshape(shape).astype(BF16)


def _prep_w_ukv(w_ukv):
    L, lora, _ = w_ukv.shape
    w = w_ukv.reshape(L, lora, MLA_HEADS, MLA_NOPE + MLA_V)
    k_nope, v = w[..., :MLA_NOPE], w[..., MLA_NOPE:]
    z = jnp.zeros(w.shape[:3] + (MLA_QK_PAD - MLA_NOPE,), w.dtype)
    wk = jnp.concatenate([k_nope, z], axis=-1).reshape(L, lora, MLA_HEADS * MLA_QK_PAD)
    wv = v.reshape(L, lora, MLA_HEADS * MLA_V)
    return wk.astype(BF16), wv.astype(BF16)


def _prep_rg_gate(w):
    L, nb, bw, _ = w.shape
    per = RG_GROUP // bw
    w = w.reshape(L, nb // per, per, bw, bw)
    eye = jnp.eye(per, dtype=w.dtype)
    tiles = jnp.einsum('lgpjk,pq->lgpjqk', w, eye)
    return tiles.reshape(L, nb // per, RG_GROUP, RG_GROUP).astype(BF16)


def kernel(x, positions, ffn1_norm, ffn1_w_gate_up, ffn1_w_down, mix_norm, w_in, conv_w, conv_b, rg_w_a, rg_b_a, rg_w_x, rg_b_x, rg_lambda, mla_q_norm, mla_w_uq, mla_kv_norm, mla_w_ukv, w_branch_a, w_branch_b, w_branch_c, w_out, ffn2_norm, ffn2_w_gate_up, ffn2_w_down, final_norm):
    B, S, D = x.shape
    depth = w_in.shape[0]
    lora = mla_q_norm.shape[1]
    T = B * S
    tm = min(512, T)
    tm_in = min(256, T)
    ts = min(256, S)
    blk = min(ATTN_BLOCK, S)
    sb_blk = min(SB_BLOCK, S)

    ffn1_wgu = ffn1_w_gate_up.astype(BF16)
    ffn1_wd = ffn1_w_down.astype(BF16)
    ffn2_wgu = ffn2_w_gate_up.astype(BF16)
    ffn2_wd = ffn2_w_down.astype(BF16)
    w_in_r = _prep_w_in(w_in, D, lora)
    wq, wq_sw = _prep_w_uq(mla_w_uq)
    wk, wv = _prep_w_ukv(mla_w_ukv)
    rg_wa = _prep_rg_gate(rg_w_a)
    rg_wx = _prep_rg_gate(rg_w_x)
    wba = w_branch_a.astype(BF16)
    wbb = w_branch_b.astype(BF16)
    wbc = w_branch_c.astype(BF16)
    wo = w_out.astype(BF16)

    cs, sn = _rope_tables(positions, tm)
    xf = x.reshape(T, D)
    rows = lambda a: a[:, None, :]
    ffn1_g, mix_g, ffn2_g = rows(ffn1_norm), rows(mix_norm), rows(ffn2_norm)
    cb, ba, bx, lam = rows(conv_b), rows(rg_b_a), rows(rg_b_x), rows(rg_lambda)
    qn, kvn = rows(mla_q_norm), rows(mla_kv_norm)

    for l in range(depth):
        xf = _ffn(xf, ffn1_g, ffn1_wgu, ffn1_wd, l, tm=tm)
        rg_x, rg_g, sb_q, sb_k, sb_v, c_q, c_kv, gates, kr = _inproj(
            xf, mix_g, w_in_r, cs, sn, l, tm=tm_in, lora=lora)
        y_a = _rglru(rg_x, rg_g, conv_w, cb, rg_wa, ba, rg_wx, bx, lam, l, batch=B, ts=ts)
        y_b = _sb_attention(sb_q, sb_k, sb_v, batch=B, blk=sb_blk, hp=SB_HEADS_PER_STEP)
        mq, mk, mv = _mla_prep(c_q, c_kv, kr, cs, sn, qn, kvn, wq, wq_sw, wk, wv, l, tm=tm)
        y_c = _mla_attention(mq, mk, mv, batch=B, blk=blk, hp=MLA_HEADS_PER_STEP)
        xf = _merge(xf, y_a, y_b, y_c, gates, wba, wbb, wbc, wo, l, tm=tm)
        xf = _ffn(xf, ffn2_g, ffn2_wgu, ffn2_wd, l,
                  final_norm[None, :] if l == depth - 1 else None, tm=tm)

    return xf.reshape(B, S, D)
```

```python
import functools
import math

import jax
import jax.numpy as jnp
from jax import lax
from jax.experimental import pallas as pl
from jax.experimental.pallas import tpu as pltpu

F32 = jnp.float32
BF16 = jnp.bfloat16

NORM_EPS = 1e-6
CHUNK = 64
RG_C = 8.0
CONV_W = 4
SB_HEAD_DIM = 128
MLA_HEADS = 8
MLA_NOPE = 128
MLA_ROPE = 64
MLA_V = 128
MLA_QK_PAD = 256
ROPE_THETA = 10000.0

V7X_VMEM_LIMIT_BYTES = 56 * 1024 * 1024
RG_GROUP = 256
SUBLANES = 8
LOG2_E = math.log2(math.e)
SB_DEAD_LOG2 = -105.0 * LOG2_E
SB_HEADS_PER_STEP = 4
SB_BLOCK = 256
ATTN_BLOCK = 512

_DN_NT = (((1,), (1,)), ((), ()))


def _cparams(sem):
    return pltpu.CompilerParams(dimension_semantics=sem,
                                vmem_limit_bytes=V7X_VMEM_LIMIT_BYTES)


def _const_spec(shape):
    nd = len(shape)
    return pl.BlockSpec(shape, lambda *_: (0,) * nd, pipeline_mode=pl.Buffered(1))


def _layer_spec(shape, l):
    nd = len(shape)
    return pl.BlockSpec((None,) + tuple(shape), lambda *_: (l,) + (0,) * nd,
                        pipeline_mode=pl.Buffered(1))


def _rms(x, g):
    return x * lax.rsqrt(jnp.mean(x * x, axis=-1, keepdims=True) + NORM_EPS) * g


def _rope_kernel(pos_ref, inv_ref, cs_ref, sn_ref):
    ang = pos_ref[...] * inv_ref[...]
    lane = lax.broadcasted_iota(jnp.int32, ang.shape, 1)
    half = MLA_ROPE // 2
    c = jnp.cos(ang)
    s = jnp.sin(ang)
    in_rope = (lane >= MLA_NOPE) & (lane < MLA_NOPE + MLA_ROPE)
    cs_ref[...] = jnp.where(lane < MLA_NOPE, 1.0, jnp.where(in_rope, c, 0.0))
    sn_ref[...] = jnp.where(in_rope, jnp.where(lane < MLA_NOPE + half, -s, s), 0.0)


def _rope_tables(positions, tm):
    T = positions.size
    pos = positions.reshape(T, 1).astype(F32)
    inv = ROPE_THETA ** (-jnp.arange(0, MLA_ROPE, 2, dtype=F32) / MLA_ROPE)
    inv_pad = jnp.concatenate([jnp.zeros((MLA_NOPE,), F32), inv, inv,
                               jnp.zeros((MLA_QK_PAD - MLA_NOPE - MLA_ROPE,), F32)])[None, :]
    return pl.pallas_call(
        _rope_kernel,
        grid=(T // tm,),
        in_specs=[pl.BlockSpec((tm, 1), lambda i: (i, 0)),
                  pl.BlockSpec((1, MLA_QK_PAD), lambda i: (0, 0))],
        out_specs=[pl.BlockSpec((tm, MLA_QK_PAD), lambda i: (i, 0))] * 2,
        out_shape=[jax.ShapeDtypeStruct((T, MLA_QK_PAD), F32)] * 2,
        compiler_params=_cparams(("parallel",)),
        name="rope_tables",
    )(pos, inv_pad)


def _ffn_kernel(x_ref, g_ref, wgu_ref, wd_ref, *rest, d_ff, n_chunk, final):
    if final:
        fg_ref, o_ref, a_ref = rest
    else:
        o_ref, a_ref = rest
    x = x_ref[...]
    h = _rms(x, g_ref[...]).astype(BF16)
    tf = d_ff // n_chunk
    for c in range(n_chunk):
        gate = jnp.dot(h, wgu_ref[:, c * tf:(c + 1) * tf], preferred_element_type=F32)
        up = jnp.dot(h, wgu_ref[:, d_ff + c * tf:d_ff + (c + 1) * tf], preferred_element_type=F32)
        a_ref[:, c * tf:(c + 1) * tf] = (gate * jax.nn.sigmoid(gate) * up).astype(BF16)
    y = x + 0.5 * jnp.dot(a_ref[...], wd_ref[...], preferred_element_type=F32)
    if final:
        y = _rms(y, fg_ref[...])
    o_ref[...] = y


def _ffn(x, g, wgu, wd, l, final_g=None, *, tm):
    T, D = x.shape
    d_ff = wd.shape[1]
    n_chunk = 2 if (d_ff // 2) % 128 == 0 else 1
    final = final_g is not None
    in_specs = [pl.BlockSpec((tm, D), lambda i: (i, 0)),
                _layer_spec((1, D), l),
                _layer_spec((D, 2 * d_ff), l),
                _layer_spec((d_ff, D), l)]
    args = [x, g, wgu, wd]
    if final:
        in_specs.append(_const_spec((1, D)))
        args.append(final_g)
    return pl.pallas_call(
        functools.partial(_ffn_kernel, d_ff=d_ff, n_chunk=n_chunk, final=final),
        grid=(T // tm,),
        in_specs=in_specs,
        out_specs=pl.BlockSpec((tm, D), lambda i: (i, 0)),
        out_shape=jax.ShapeDtypeStruct((T, D), F32),
        scratch_shapes=[pltpu.VMEM((tm, d_ff), BF16)],
        compiler_params=_cparams(("parallel",)),
        name="ffn_final" if final else "ffn",
    )(*args)


def _inproj_kernel(x_ref, g_ref, w_ref, wg_ref, wkr_ref, cs_ref, sn_ref,
                   rgx_ref, rgg_ref, q_ref, k_ref, v_ref, cq_ref, ckv_ref, gate_ref, kr_ref,
                   *, d, lora, sb_scale):
    h = _rms(x_ref[...], g_ref[...]).astype(BF16)

    def proj(ref, lo, width):
        return jnp.dot(h, ref[:, lo:lo + width], preferred_element_type=F32)

    o = 0
    rgx_ref[...] = proj(w_ref, o, d); o += d
    rgg_ref[...] = proj(w_ref, o, d); o += d
    q_ref[...] = (proj(w_ref, o, d) * sb_scale).astype(BF16); o += d
    k_ref[...] = proj(w_ref, o, d).astype(BF16); o += d
    v_ref[...] = proj(w_ref, o, d).astype(BF16); o += d
    cq_ref[...] = proj(w_ref, o, lora); o += lora
    ckv_ref[...] = proj(w_ref, o, lora)
    gate_ref[...] = jax.nn.sigmoid(proj(wg_ref, 0, 3 * d))
    kr = proj(wkr_ref, 0, MLA_QK_PAD)
    kr_sw = proj(wkr_ref, MLA_QK_PAD, MLA_QK_PAD)
    kr_ref[...] = kr * cs_ref[...] + kr_sw * sn_ref[...]


def _inproj(x, g, w, w_gate, w_kr, cs, sn, l, *, tm, lora):
    T, D = x.shape
    n_lead = 5 * D + 2 * lora
    row = lambda width: pl.BlockSpec((tm, width), lambda i: (i, 0))
    sds = lambda width, dt: jax.ShapeDtypeStruct((T, width), dt)
    return pl.pallas_call(
        functools.partial(_inproj_kernel, d=D, lora=lora, sb_scale=SB_HEAD_DIM ** -0.5 * LOG2_E),
        grid=(T // tm,),
        in_specs=[row(D), _layer_spec((1, D), l), _layer_spec((D, n_lead), l),
                  _layer_spec((D, 3 * D), l), _layer_spec((D, 2 * MLA_QK_PAD), l),
                  row(MLA_QK_PAD), row(MLA_QK_PAD)],
        out_specs=[row(D), row(D), row(D), row(D), row(D), row(lora), row(lora),
                   row(3 * D), row(MLA_QK_PAD)],
        out_shape=[sds(D, F32), sds(D, F32), sds(D, BF16), sds(D, BF16), sds(D, BF16),
                   sds(lora, F32), sds(lora, F32), sds(3 * D, F32), sds(MLA_QK_PAD, F32)],
        compiler_params=_cparams(("parallel",)),
        name="in_proj",
    )(x, g, w, w_gate, w_kr, cs, sn)


def _rglru_kernel(x_ref, gate_ref, cw_ref, cb_ref, wa_ref, ba_ref, wx_ref, bx_ref, lam_ref,
                  o_ref, xbuf, hprev, *, ts, n_group):
    s = pl.program_id(1)
    halo = SUBLANES

    @pl.when(s == 0)
    def _():
        xbuf[0:halo, :] = jnp.zeros((halo, xbuf.shape[1]), F32)
        hprev[...] = jnp.zeros(hprev.shape, F32)

    xbuf[halo:halo + ts, :] = x_ref[...]
    n_sub = ts // SUBLANES
    sub = lax.broadcasted_iota(jnp.int32, (n_sub, SUBLANES, RG_GROUP), 1)

    for gi in range(n_group):
        sl = slice(gi * RG_GROUP, (gi + 1) * RG_GROUP)
        u = cb_ref[:, sl] + cw_ref[CONV_W - 1:CONV_W, sl] * xbuf[halo:halo + ts, sl]
        for back in range(1, CONV_W):
            tap = CONV_W - 1 - back
            u = u + cw_ref[tap:tap + 1, sl] * xbuf[halo - back:halo - back + ts, sl]
        ub = u.astype(BF16)
        r = jax.nn.sigmoid(jnp.dot(ub, wa_ref[gi], preferred_element_type=F32) + ba_ref[:, sl])
        i_gate = jax.nn.sigmoid(jnp.dot(ub, wx_ref[gi], preferred_element_type=F32) + bx_ref[:, sl])
        lam = lam_ref[:, sl]
        neg_softplus = -(jnp.maximum(-lam, 0.0) + jnp.log1p(jnp.exp(-jnp.abs(lam))))
        log_a = (RG_C * neg_softplus) * r
        a = jnp.exp(log_a)
        b = jnp.sqrt(-jnp.tanh(log_a) * (a * a + 1.0)) * (i_gate * u)
        a = a.reshape(n_sub, SUBLANES, RG_GROUP)
        b = b.reshape(n_sub, SUBLANES, RG_GROUP)
        d = 1
        while d < SUBLANES:
            m = sub >= d
            a_s = pltpu.roll(a, d, 1)
            b_s = pltpu.roll(b, d, 1)
            b = jnp.where(m, a * b_s + b, b)
            a = jnp.where(m, a * a_s, a)
            d *= 2
        state = hprev[:, sl]
        groups = []
        for g in range(n_sub):
            hg = a[g] * state + b[g]
            state = hg[SUBLANES - 1:SUBLANES, :]
            groups.append(hg)
        hs = jnp.concatenate(groups, axis=0)
        hprev[:, sl] = state
        o_ref[:, sl] = (hs * jax.nn.gelu(gate_ref[:, sl], approximate=True)).astype(o_ref.dtype)

    xbuf[0:halo, :] = x_ref[ts - halo:ts, :]


def _rglru(rg_x, rg_g, cw, cb, wa, ba, wx, bx, lam, l, *, batch, ts):
    T, C = rg_x.shape
    S = T // batch
    ns = S // ts
    n_group = C // RG_GROUP
    row = pl.BlockSpec((ts, C), lambda b, s: (b * ns + s, 0))
    return pl.pallas_call(
        functools.partial(_rglru_kernel, ts=ts, n_group=n_group),
        grid=(batch, ns),
        in_specs=[row, row,
                  _layer_spec((CONV_W, C), l), _layer_spec((1, C), l),
                  _layer_spec((n_group, RG_GROUP, RG_GROUP), l), _layer_spec((1, C), l),
                  _layer_spec((n_group, RG_GROUP, RG_GROUP), l), _layer_spec((1, C), l),
                  _layer_spec((1, C), l)],
        out_specs=row,
        out_shape=jax.ShapeDtypeStruct((T, C), BF16),
        scratch_shapes=[pltpu.VMEM((ts + SUBLANES, C), F32), pltpu.VMEM((1, C), F32)],
        compiler_params=_cparams(("parallel", "arbitrary")),
        name="rglru",
    )(rg_x, rg_g, cw, cb, wa, ba, wx, bx, lam)


def _sb_kernel(q_ref, k_ref, v_ref, tri_ref, o_ref, *, blk, nblk, hp):
    rowi = lax.broadcasted_iota(jnp.int32, (blk, blk), 0)
    coli = lax.broadcasted_iota(jnp.int32, (blk, blk), 1)
    earlier = coli < rowi
    hd = SB_HEAD_DIM

    def load(ref, j, h):
        r0 = j * blk if isinstance(j, int) else pl.multiple_of(j * blk, blk)
        return ref[pl.ds(r0, blk), h * hd:(h + 1) * hd]

    def block_terms(q, j, h, diag):
        z = lax.dot_general(q, load(k_ref, j, h), _DN_NT, preferred_element_type=F32)
        tail = jnp.log(1.0 + jnp.exp2(-jnp.abs(z))) * LOG2_E
        log_beta = jnp.minimum(z, 0.0) - tail
        log_keep = log_beta - z
        if diag:
            log_keep = jnp.where(earlier, log_keep, 0.0)
        hi = log_keep.astype(BF16)
        lo = (log_keep - hi.astype(F32)).astype(BF16)
        tri = tri_ref[...]
        between = (jnp.dot(hi, tri, preferred_element_type=F32)
                   + jnp.dot(lo, tri, preferred_element_type=F32))
        return log_beta + between, jnp.sum(log_keep, axis=1, keepdims=True)

    def pv(w, j, h):
        return jnp.dot(w.astype(BF16), load(v_ref, j, h), preferred_element_type=F32)

    def diag_block(i, h):
        q = load(q_ref, i, h)
        logw, kept = block_terms(q, i, h, True)
        w = jnp.where(earlier, jnp.exp2(logw), 0.0)
        return q, pv(w, i, h), kept

    def any_live(rems):
        top = rems[0]
        for r in rems[1:]:
            top = jnp.maximum(top, r)
        return jnp.max(top) > SB_DEAD_LOG2

    for h in range(hp):
        _, acc, _ = diag_block(0, h)
        o_ref[0:blk, h * hd:(h + 1) * hd] = acc.astype(o_ref.dtype)

    def q_body(i, carry):
        accs, rems = [], []
        for h in range(hp):
            q, acc, kept_d = diag_block(i, h)
            logw, kept_p = block_terms(q, i - 1, h, False)
            accs.append(acc + pv(jnp.exp2(logw + kept_d), i - 1, h))
            rems.append(kept_d + kept_p)

        def cond(c):
            j, live, _, _ = c
            return jnp.logical_and(j >= 0, live)

        def body(c):
            j, _, accs, rems = c
            new_accs, new_rems = [], []
            for h in range(hp):
                logw, kept = block_terms(load(q_ref, i, h), j, h, False)
                new_accs.append(accs[h] + pv(jnp.exp2(logw + rems[h]), j, h))
                new_rems.append(rems[h] + kept)
            return j - 1, any_live(new_rems), tuple(new_accs), tuple(new_rems)

        _, _, accs, _ = lax.while_loop(cond, body, (i - 2, any_live(rems), tuple(accs), tuple(rems)))
        q0 = pl.multiple_of(i * blk, blk)
        for h in range(hp):
            o_ref[pl.ds(q0, blk), h * hd:(h + 1) * hd] = accs[h].astype(o_ref.dtype)
        return carry

    lax.fori_loop(1, nblk, q_body, 0)


def _sb_attention(q, k, v, *, batch, blk, hp):
    T, W = q.shape
    S = T // batch
    heads = W // SB_HEAD_DIM
    tri = (lax.broadcasted_iota(jnp.int32, (blk, blk), 0)
           > lax.broadcasted_iota(jnp.int32, (blk, blk), 1)).astype(BF16)
    spec = pl.BlockSpec((S, hp * SB_HEAD_DIM), lambda b, h: (b, h))
    return pl.pallas_call(
        functools.partial(_sb_kernel, blk=blk, nblk=S // blk, hp=hp),
        grid=(batch, heads // hp),
        in_specs=[spec, spec, spec, _const_spec((blk, blk))],
        out_specs=spec,
        out_shape=jax.ShapeDtypeStruct((T, W), BF16),
        compiler_params=_cparams(("parallel", "parallel")),
        name="sb_attention",
    )(q, k, v, tri)


def _mla_prep_kernel(cq_ref, ckv_ref, kr_ref, cs_ref, sn_ref, qn_ref, kvn_ref,
                     wq_ref, wqs_ref, wk_ref, wv_ref, q_ref, k_ref, v_ref, *, scale):
    cq = _rms(cq_ref[...], qn_ref[...]).astype(BF16)
    ckv = _rms(ckv_ref[...], kvn_ref[...]).astype(BF16)
    cs = cs_ref[...]
    sn = sn_ref[...]
    kr = kr_ref[...]
    v_ref[...] = jnp.dot(ckv, wv_ref[...], preferred_element_type=F32).astype(BF16)
    for h in range(MLA_HEADS):
        sl = slice(h * MLA_QK_PAD, (h + 1) * MLA_QK_PAD)
        qh = jnp.dot(cq, wq_ref[:, sl], preferred_element_type=F32)
        qh_sw = jnp.dot(cq, wqs_ref[:, sl], preferred_element_type=F32)
        q_ref[:, sl] = ((qh * cs + qh_sw * sn) * scale).astype(BF16)
        kh = jnp.dot(ckv, wk_ref[:, sl], preferred_element_type=F32)
        k_ref[:, sl] = (kh + kr).astype(BF16)


def _mla_prep(cq, ckv, kr, cs, sn, qn, kvn, wq, wqs, wk, wv, l, *, tm):
    T, lora = cq.shape
    qk_w = MLA_HEADS * MLA_QK_PAD
    v_w = MLA_HEADS * MLA_V
    row = lambda width: pl.BlockSpec((tm, width), lambda i: (i, 0))
    return pl.pallas_call(
        functools.partial(_mla_prep_kernel, scale=(MLA_NOPE + MLA_ROPE) ** -0.5 * math.log2(math.e)),
        grid=(T // tm,),
        in_specs=[row(lora), row(lora), row(MLA_QK_PAD), row(MLA_QK_PAD), row(MLA_QK_PAD),
                  _layer_spec((1, lora), l), _layer_spec((1, lora), l),
                  _layer_spec((lora, qk_w), l), _layer_spec((lora, qk_w), l),
                  _layer_spec((lora, qk_w), l), _layer_spec((lora, v_w), l)],
        out_specs=[row(qk_w), row(qk_w), row(v_w)],
        out_shape=[jax.ShapeDtypeStruct((T, qk_w), BF16),
                   jax.ShapeDtypeStruct((T, qk_w), BF16),
                   jax.ShapeDtypeStruct((T, v_w), BF16)],
        compiler_params=_cparams(("parallel",)),
        name="mla_prep",
    )(cq, ckv, kr, cs, sn, qn, kvn, wq, wqs, wk, wv)


def _mla_kernel(q_ref, k_ref, v_ref, o_ref, *, blk, nblk):
    rowc = lax.broadcasted_iota(jnp.int32, (blk, blk), 0) // CHUNK
    colc = lax.broadcasted_iota(jnp.int32, (blk, blk), 1) // CHUNK
    visible = colc <= rowc
    lanes = MLA_V

    def fold(a, op):
        parts = [a[:, c * lanes:(c + 1) * lanes] for c in range(a.shape[1] // lanes)]
        while len(parts) > 1:
            nxt = [op(parts[n], parts[n + 1]) for n in range(0, len(parts) - 1, 2)]
            parts = nxt + parts[len(parts) - len(parts) % 2:]
        return parts[0]

    for i in range(nblk):
        rows = slice(i * blk, (i + 1) * blk)
        q = q_ref[rows, :]
        s_diag = jnp.where(visible, lax.dot_general(q, k_ref[rows, :], _DN_NT, preferred_element_type=F32),
                           -jnp.inf)
        m_lane = fold(s_diag, jnp.maximum)
        if i > 0:
            s_past = lax.dot_general(q, k_ref[0:i * blk, :], _DN_NT, preferred_element_type=F32)
            m_lane = jnp.maximum(m_lane, fold(s_past, jnp.maximum))
        m = jnp.max(m_lane, axis=1, keepdims=True)
        p = jnp.exp2(s_diag - m)
        l_lane = fold(p, jnp.add)
        acc = jnp.dot(p.astype(BF16), v_ref[rows, :], preferred_element_type=F32)
        if i > 0:
            p = jnp.exp2(s_past - m)
            l_lane = l_lane + fold(p, jnp.add)
            acc = acc + jnp.dot(p.astype(BF16), v_ref[0:i * blk, :], preferred_element_type=F32)
        l = jnp.sum(l_lane, axis=1, keepdims=True)
        o_ref[rows, :] = (acc / l).astype(o_ref.dtype)


def _mla_attention(q, k, v, *, batch, blk):
    T = q.shape[0]
    S = T // batch
    qk_spec = pl.BlockSpec((S, MLA_QK_PAD), lambda b, h: (b, h))
    v_spec = pl.BlockSpec((S, MLA_V), lambda b, h: (b, h))
    return pl.pallas_call(
        functools.partial(_mla_kernel, blk=blk, nblk=S // blk),
        grid=(batch, MLA_HEADS),
        in_specs=[qk_spec, qk_spec, v_spec],
        out_specs=v_spec,
        out_shape=jax.ShapeDtypeStruct((T, MLA_HEADS * MLA_V), BF16),
        compiler_params=_cparams(("parallel", "parallel")),
        name="mla_attention",
    )(q, k, v)


def _merge_kernel(x_ref, ya_ref, yb_ref, yc_ref, gate_ref, wa_ref, wb_ref, wc_ref, wo_ref, o_ref, *, d):
    merged = gate_ref[:, 0:d] * jnp.dot(ya_ref[...], wa_ref[...], preferred_element_type=F32)
    merged = merged + gate_ref[:, d:2 * d] * jnp.dot(yb_ref[...], wb_ref[...], preferred_element_type=F32)
    merged = merged + gate_ref[:, 2 * d:3 * d] * jnp.dot(yc_ref[...], wc_ref[...], preferred_element_type=F32)
    o_ref[...] = x_ref[...] + jnp.dot(merged.astype(BF16), wo_ref[...], preferred_element_type=F32)


def _merge(x, ya, yb, yc, gates, wa, wb, wc, wo, l, *, tm):
    T, D = x.shape
    row = lambda width: pl.BlockSpec((tm, width), lambda i: (i, 0))
    wspec = _layer_spec((D, D), l)
    return pl.pallas_call(
        functools.partial(_merge_kernel, d=D),
        grid=(T // tm,),
        in_specs=[row(D), row(D), row(D), row(D), row(3 * D), wspec, wspec, wspec, wspec],
        out_specs=row(D),
        out_shape=jax.ShapeDtypeStruct((T, D), F32),
        compiler_params=_cparams(("parallel",)),
        name="merge",
    )(x, ya, yb, yc, gates, wa, wb, wc, wo)


def _prep_w_in(w_in, d, lora):
    w = w_in.astype(BF16)
    kr_lo = 5 * d + 2 * lora
    kr_hi = kr_lo + MLA_ROPE
    half = MLA_ROPE // 2
    kr = w[:, :, kr_lo:kr_hi]
    lead = w.shape[:2]
    z_nope = jnp.zeros(lead + (MLA_NOPE,), BF16)
    z_tail = jnp.zeros(lead + (MLA_QK_PAD - MLA_NOPE - MLA_ROPE,), BF16)
    kr_pads = jnp.concatenate([z_nope, kr[..., :half], kr[..., half:], z_tail,
                               z_nope, kr[..., half:], kr[..., :half], z_tail], axis=-1)
    return w, w[:, :, kr_hi:], kr_pads


def _prep_w_uq(w_uq):
    L, lora, _ = w_uq.shape
    half = MLA_ROPE // 2
    w = w_uq.reshape(L, lora, MLA_HEADS, MLA_NOPE + MLA_ROPE)
    nope, r1, r2 = w[..., :MLA_NOPE], w[..., MLA_NOPE:MLA_NOPE + half], w[..., MLA_NOPE + half:]
    z_tail = jnp.zeros(w.shape[:3] + (MLA_QK_PAD - MLA_NOPE - MLA_ROPE,), w.dtype)
    plain = jnp.concatenate([nope, r1, r2, z_tail], axis=-1)
    swapped = jnp.concatenate([jnp.zeros_like(nope), r2, r1, z_tail], axis=-1)
    shape = (L, lora, MLA_HEADS * MLA_QK_PAD)
    return plain.reshape(shape).astype(BF16), swapped.reshape(shape).astype(BF16)


def _prep_w_ukv(w_ukv):
    L, lora, _ = w_ukv.shape
    w = w_ukv.reshape(L, lora, MLA_HEADS, MLA_NOPE + MLA_V)
    k_nope, v = w[..., :MLA_NOPE], w[..., MLA_NOPE:]
    z = jnp.zeros(w.shape[:3] + (MLA_QK_PAD - MLA_NOPE,), w.dtype)
    wk = jnp.concatenate([k_nope, z], axis=-1).reshape(L, lora, MLA_HEADS * MLA_QK_PAD)
    wv = v.reshape(L, lora, MLA_HEADS * MLA_V)
    return wk.astype(BF16), wv.astype(BF16)


def _prep_rg_gate(w):
    L, nb, bw, _ = w.shape
    per = RG_GROUP // bw
    w = w.reshape(L, nb // per, per, bw, bw)
    eye = jnp.eye(per, dtype=w.dtype)
    tiles = jnp.einsum('lgpjk,pq->lgpjqk', w, eye)
    return tiles.reshape(L, nb // per, RG_GROUP, RG_GROUP).astype(BF16)


def kernel(x, positions, ffn1_norm, ffn1_w_gate_up, ffn1_w_down, mix_norm, w_in, conv_w, conv_b, rg_w_a, rg_b_a, rg_w_x, rg_b_x, rg_lambda, mla_q_norm, mla_w_uq, mla_kv_norm, mla_w_ukv, w_branch_a, w_branch_b, w_branch_c, w_out, ffn2_norm, ffn2_w_gate_up, ffn2_w_down, final_norm):
    B, S, D = x.shape
    depth = w_in.shape[0]
    lora = mla_q_norm.shape[1]
    T = B * S
    tm = min(512, T)
    tm_in = min(256, T)
    ts = min(256, S)
    blk = min(ATTN_BLOCK, S)
    sb_blk = min(SB_BLOCK, S)

    ffn1_wgu = ffn1_w_gate_up.astype(BF16)
    ffn1_wd = ffn1_w_down.astype(BF16)
    ffn2_wgu = ffn2_w_gate_up.astype(BF16)
    ffn2_wd = ffn2_w_down.astype(BF16)
    w_in_b, w_gate, w_kr = _prep_w_in(w_in, D, lora)
    wq, wq_sw = _prep_w_uq(mla_w_uq)
    wk, wv = _prep_w_ukv(mla_w_ukv)
    rg_wa = _prep_rg_gate(rg_w_a)
    rg_wx = _prep_rg_gate(rg_w_x)
    wba = w_branch_a.astype(BF16)
    wbb = w_branch_b.astype(BF16)
    wbc = w_branch_c.astype(BF16)
    wo = w_out.astype(BF16)

    cs, sn = _rope_tables(positions, tm)
    xf = x.reshape(T, D)
    rows = lambda a: a[:, None, :]
    ffn1_g, mix_g, ffn2_g = rows(ffn1_norm), rows(mix_norm), rows(ffn2_norm)
    cb, ba, bx, lam = rows(conv_b), rows(rg_b_a), rows(rg_b_x), rows(rg_lambda)
    qn, kvn = rows(mla_q_norm), rows(mla_kv_norm)

    for l in range(depth):
        xf = _ffn(xf, ffn1_g, ffn1_wgu, ffn1_wd, l, tm=tm)
        rg_x, rg_g, sb_q, sb_k, sb_v, c_q, c_kv, gates, kr = _inproj(
            xf, mix_g, w_in_b, w_gate, w_kr, cs, sn, l, tm=tm_in, lora=lora)
        y_a = _rglru(rg_x, rg_g, conv_w, cb, rg_wa, ba, rg_wx, bx, lam, l, batch=B, ts=ts)
        y_b = _sb_attention(sb_q, sb_k, sb_v, batch=B, blk=sb_blk, hp=SB_HEADS_PER_STEP)
        mq, mk, mv = _mla_prep(c_q, c_kv, kr, cs, sn, qn, kvn, wq, wq_sw, wk, wv, l, tm=tm)
        y_c = _mla_attention(mq, mk, mv, batch=B, blk=blk)
        xf = _merge(xf, y_a, y_b, y_c, gates, wba, wbb, wbc, wo, l, tm=tm)
        xf = _ffn(xf, ffn2_g, ffn2_wgu, ffn2_wd, l,
                  final_norm[None, :] if l == depth - 1 else None, tm=tm)

    return xf.reshape(B, S, D)
```

```python
import functools
import math

import jax
import jax.numpy as jnp
from jax import lax
from jax.experimental import pallas as pl
from jax.experimental.pallas import tpu as pltpu

F32 = jnp.float32
BF16 = jnp.bfloat16

NORM_EPS = 1e-6
CHUNK = 64
RG_C = 8.0
CONV_W = 4
SB_HEAD_DIM = 128
MLA_HEADS = 8
MLA_NOPE = 128
MLA_ROPE = 64
MLA_V = 128
MLA_QK_PAD = 256
ROPE_THETA = 10000.0

V7X_VMEM_LIMIT_BYTES = 56 * 1024 * 1024
RG_GROUP = 256
SUBLANES = 8
LOG2_E = math.log2(math.e)
SB_DEAD_LOG2 = -105.0 * LOG2_E
SB_HEADS_PER_STEP = 4
SB_BLOCK = 256
ATTN_BLOCK = 512

_DN_NT = (((1,), (1,)), ((), ()))


def _cparams(sem):
    return pltpu.CompilerParams(dimension_semantics=sem,
                                vmem_limit_bytes=V7X_VMEM_LIMIT_BYTES)


def _const_spec(shape):
    nd = len(shape)
    return pl.BlockSpec(shape, lambda *_: (0,) * nd, pipeline_mode=pl.Buffered(1))


def _layer_spec(shape, l):
    nd = len(shape)
    return pl.BlockSpec((None,) + tuple(shape), lambda *_: (l,) + (0,) * nd,
                        pipeline_mode=pl.Buffered(1))


def _rms(x, g):
    return x * lax.rsqrt(jnp.mean(x * x, axis=-1, keepdims=True) + NORM_EPS) * g


def _rope_kernel(pos_ref, inv_ref, cs_ref, sn_ref):
    ang = pos_ref[...] * inv_ref[...]
    lane = lax.broadcasted_iota(jnp.int32, ang.shape, 1)
    half = MLA_ROPE // 2
    c = jnp.cos(ang)
    s = jnp.sin(ang)
    in_rope = (lane >= MLA_NOPE) & (lane < MLA_NOPE + MLA_ROPE)
    cs_ref[...] = jnp.where(lane < MLA_NOPE, 1.0, jnp.where(in_rope, c, 0.0))
    sn_ref[...] = jnp.where(in_rope, jnp.where(lane < MLA_NOPE + half, -s, s), 0.0)


def _rope_tables(positions, tm):
    T = positions.size
    pos = positions.reshape(T, 1).astype(F32)
    inv = ROPE_THETA ** (-jnp.arange(0, MLA_ROPE, 2, dtype=F32) / MLA_ROPE)
    inv_pad = jnp.concatenate([jnp.zeros((MLA_NOPE,), F32), inv, inv,
                               jnp.zeros((MLA_QK_PAD - MLA_NOPE - MLA_ROPE,), F32)])[None, :]
    return pl.pallas_call(
        _rope_kernel,
        grid=(T // tm,),
        in_specs=[pl.BlockSpec((tm, 1), lambda i: (i, 0)),
                  pl.BlockSpec((1, MLA_QK_PAD), lambda i: (0, 0))],
        out_specs=[pl.BlockSpec((tm, MLA_QK_PAD), lambda i: (i, 0))] * 2,
        out_shape=[jax.ShapeDtypeStruct((T, MLA_QK_PAD), F32)] * 2,
        compiler_params=_cparams(("parallel",)),
        name="rope_tables",
    )(pos, inv_pad)


def _ffn_kernel(x_ref, g_ref, wgu_ref, wd_ref, *rest, d_ff, n_chunk, final):
    if final:
        fg_ref, o_ref, a_ref = rest
    else:
        o_ref, a_ref = rest
    x = x_ref[...]
    h = _rms(x, g_ref[...]).astype(BF16)
    tf = d_ff // n_chunk
    for c in range(n_chunk):
        gate = jnp.dot(h, wgu_ref[:, c * tf:(c + 1) * tf], preferred_element_type=F32)
        up = jnp.dot(h, wgu_ref[:, d_ff + c * tf:d_ff + (c + 1) * tf], preferred_element_type=F32)
        a_ref[:, c * tf:(c + 1) * tf] = (gate * jax.nn.sigmoid(gate) * up).astype(BF16)
    y = x + 0.5 * jnp.dot(a_ref[...], wd_ref[...], preferred_element_type=F32)
    if final:
        y = _rms(y, fg_ref[...])
    o_ref[...] = y


def _ffn(x, g, wgu, wd, l, final_g=None, *, tm):
    T, D = x.shape
    d_ff = wd.shape[1]
    n_chunk = 2 if (d_ff // 2) % 128 == 0 else 1
    final = final_g is not None
    in_specs = [pl.BlockSpec((tm, D), lambda i: (i, 0)),
                _layer_spec((1, D), l),
                _layer_spec((D, 2 * d_ff), l),
                _layer_spec((d_ff, D), l)]
    args = [x, g, wgu, wd]
    if final:
        in_specs.append(_const_spec((1, D)))
        args.append(final_g)
    return pl.pallas_call(
        functools.partial(_ffn_kernel, d_ff=d_ff, n_chunk=n_chunk, final=final),
        grid=(T // tm,),
        in_specs=in_specs,
        out_specs=pl.BlockSpec((tm, D), lambda i: (i, 0)),
        out_shape=jax.ShapeDtypeStruct((T, D), F32),
        scratch_shapes=[pltpu.VMEM((tm, d_ff), BF16)],
        compiler_params=_cparams(("parallel",)),
        name="ffn_final" if final else "ffn",
    )(*args)


def _inproj_kernel(x_ref, g_ref, w_ref, wg_ref, wkr_ref, cs_ref, sn_ref,
                   rgx_ref, rgg_ref, q_ref, k_ref, v_ref, cq_ref, ckv_ref, gate_ref, kr_ref,
                   *, d, lora, sb_scale):
    h = _rms(x_ref[...], g_ref[...]).astype(BF16)

    def proj(ref, lo, width):
        return jnp.dot(h, ref[:, lo:lo + width], preferred_element_type=F32)

    o = 0
    rgx_ref[...] = proj(w_ref, o, d); o += d
    rgg_ref[...] = proj(w_ref, o, d); o += d
    q_ref[...] = (proj(w_ref, o, d) * sb_scale).astype(BF16); o += d
    k_ref[...] = proj(w_ref, o, d).astype(BF16); o += d
    v_ref[...] = proj(w_ref, o, d).astype(BF16); o += d
    cq_ref[...] = proj(w_ref, o, lora); o += lora
    ckv_ref[...] = proj(w_ref, o, lora)
    gate_ref[...] = jax.nn.sigmoid(proj(wg_ref, 0, 3 * d))
    kr = proj(wkr_ref, 0, MLA_QK_PAD)
    kr_sw = proj(wkr_ref, MLA_QK_PAD, MLA_QK_PAD)
    kr_ref[...] = kr * cs_ref[...] + kr_sw * sn_ref[...]


def _inproj(x, g, w, w_gate, w_kr, cs, sn, l, *, tm, lora):
    T, D = x.shape
    n_lead = 5 * D + 2 * lora
    row = lambda width: pl.BlockSpec((tm, width), lambda i: (i, 0))
    sds = lambda width, dt: jax.ShapeDtypeStruct((T, width), dt)
    return pl.pallas_call(
        functools.partial(_inproj_kernel, d=D, lora=lora, sb_scale=SB_HEAD_DIM ** -0.5 * LOG2_E),
        grid=(T // tm,),
        in_specs=[row(D), _layer_spec((1, D), l), _layer_spec((D, n_lead), l),
                  _layer_spec((D, 3 * D), l), _layer_spec((D, 2 * MLA_QK_PAD), l),
                  row(MLA_QK_PAD), row(MLA_QK_PAD)],
        out_specs=[row(D), row(D), row(D), row(D), row(D), row(lora), row(lora),
                   row(3 * D), row(MLA_QK_PAD)],
        out_shape=[sds(D, F32), sds(D, F32), sds(D, BF16), sds(D, BF16), sds(D, BF16),
                   sds(lora, F32), sds(lora, F32), sds(3 * D, F32), sds(MLA_QK_PAD, F32)],
        compiler_params=_cparams(("parallel",)),
        name="in_proj",
    )(x, g, w, w_gate, w_kr, cs, sn)


def _rglru_kernel(x_ref, gate_ref, cw_ref, cb_ref, wa_ref, ba_ref, wx_ref, bx_ref, lam_ref,
                  o_ref, xbuf, hprev, *, ts, n_group):
    s = pl.program_id(1)
    halo = SUBLANES

    @pl.when(s == 0)
    def _():
        xbuf[0:halo, :] = jnp.zeros((halo, xbuf.shape[1]), F32)
        hprev[...] = jnp.zeros(hprev.shape, F32)

    xbuf[halo:halo + ts, :] = x_ref[...]
    n_sub = ts // SUBLANES
    sub = lax.broadcasted_iota(jnp.int32, (n_sub, SUBLANES, RG_GROUP), 1)

    for gi in range(n_group):
        sl = slice(gi * RG_GROUP, (gi + 1) * RG_GROUP)
        u = cb_ref[:, sl] + cw_ref[CONV_W - 1:CONV_W, sl] * xbuf[halo:halo + ts, sl]
        for back in range(1, CONV_W):
            tap = CONV_W - 1 - back
            u = u + cw_ref[tap:tap + 1, sl] * xbuf[halo - back:halo - back + ts, sl]
        ub = u.astype(BF16)
        r = jax.nn.sigmoid(jnp.dot(ub, wa_ref[gi], preferred_element_type=F32) + ba_ref[:, sl])
        i_gate = jax.nn.sigmoid(jnp.dot(ub, wx_ref[gi], preferred_element_type=F32) + bx_ref[:, sl])
        lam = lam_ref[:, sl]
        neg_softplus = -(jnp.maximum(-lam, 0.0) + jnp.log1p(jnp.exp(-jnp.abs(lam))))
        log_a = (RG_C * neg_softplus) * r
        a = jnp.exp(log_a)
        b = jnp.sqrt(-jnp.tanh(log_a) * (a * a + 1.0)) * (i_gate * u)
        a = a.reshape(n_sub, SUBLANES, RG_GROUP)
        b = b.reshape(n_sub, SUBLANES, RG_GROUP)
        d = 1
        while d < SUBLANES:
            m = sub >= d
            a_s = pltpu.roll(a, d, 1)
            b_s = pltpu.roll(b, d, 1)
            b = jnp.where(m, a * b_s + b, b)
            a = jnp.where(m, a * a_s, a)
            d *= 2
        state = hprev[:, sl]
        groups = []
        for g in range(n_sub):
            hg = a[g] * state + b[g]
            state = hg[SUBLANES - 1:SUBLANES, :]
            groups.append(hg)
        hs = jnp.concatenate(groups, axis=0)
        hprev[:, sl] = state
        o_ref[:, sl] = (hs * jax.nn.gelu(gate_ref[:, sl], approximate=True)).astype(o_ref.dtype)

    xbuf[0:halo, :] = x_ref[ts - halo:ts, :]


def _rglru(rg_x, rg_g, cw, cb, wa, ba, wx, bx, lam, l, *, batch, ts):
    T, C = rg_x.shape
    S = T // batch
    ns = S // ts
    n_group = C // RG_GROUP
    row = pl.BlockSpec((ts, C), lambda b, s: (b * ns + s, 0))
    return pl.pallas_call(
        functools.partial(_rglru_kernel, ts=ts, n_group=n_group),
        grid=(batch, ns),
        in_specs=[row, row,
                  _layer_spec((CONV_W, C), l), _layer_spec((1, C), l),
                  _layer_spec((n_group, RG_GROUP, RG_GROUP), l), _layer_spec((1, C), l),
                  _layer_spec((n_group, RG_GROUP, RG_GROUP), l), _layer_spec((1, C), l),
                  _layer_spec((1, C), l)],
        out_specs=row,
        out_shape=jax.ShapeDtypeStruct((T, C), BF16),
        scratch_shapes=[pltpu.VMEM((ts + SUBLANES, C), F32), pltpu.VMEM((1, C), F32)],
        compiler_params=_cparams(("parallel", "arbitrary")),
        name="rglru",
    )(rg_x, rg_g, cw, cb, wa, ba, wx, bx, lam)


def _sb_kernel(q_ref, k_ref, v_ref, tri_ref, o_ref, lb_scr, hi_scr, lo_scr, kept_scr, live_ref,
               *, blk, nblk, hp):
    rowi = lax.broadcasted_iota(jnp.int32, (blk, blk), 0)
    coli = lax.broadcasted_iota(jnp.int32, (blk, blk), 1)
    earlier = coli < rowi
    hd = SB_HEAD_DIM

    def load(ref, j, h):
        r0 = j * blk if isinstance(j, int) else pl.multiple_of(j * blk, blk)
        return ref[pl.ds(r0, blk), h * hd:(h + 1) * hd]

    def logits_terms(q, j, h, diag):
        z = lax.dot_general(q, load(k_ref, j, h), _DN_NT, preferred_element_type=F32)
        tail = jnp.log(1.0 + jnp.exp2(-jnp.abs(z))) * LOG2_E
        log_beta = jnp.minimum(z, 0.0) - tail
        log_keep = log_beta - z
        if diag:
            log_keep = jnp.where(earlier, log_keep, 0.0)
        hi = log_keep.astype(BF16)
        lo = (log_keep - hi.astype(F32)).astype(BF16)
        return log_beta, hi, lo, jnp.sum(log_keep, axis=1, keepdims=True)

    def suffix_sums(hi, lo):
        tri = tri_ref[...]
        return jnp.dot(hi, tri, preferred_element_type=F32) + jnp.dot(lo, tri, preferred_element_type=F32)

    def block_terms(q, j, h, diag):
        log_beta, hi, lo, kept = logits_terms(q, j, h, diag)
        return log_beta + suffix_sums(hi, lo), kept

    def pv(w, j, h):
        return jnp.dot(w.astype(BF16), load(v_ref, j, h), preferred_element_type=F32)

    def diag_block(i, h):
        q = load(q_ref, i, h)
        logw, kept = block_terms(q, i, h, True)
        w = jnp.where(earlier, jnp.exp2(logw), 0.0)
        return q, pv(w, i, h), kept

    def any_live(rems):
        top = rems[0]
        for r in rems[1:]:
            top = jnp.maximum(top, r)
        return jnp.max(top) > SB_DEAD_LOG2

    def store(i, accs):
        q0 = i * blk if isinstance(i, int) else pl.multiple_of(i * blk, blk)
        for h in range(hp):
            o_ref[pl.ds(q0, blk), h * hd:(h + 1) * hd] = accs[h].astype(o_ref.dtype)

    def stage_one(i, slot):
        for h in range(hp):
            q = load(q_ref, i, h)
            for part, (j, diag) in enumerate(((i, True), (i - 1, False))):
                c = 2 * h + part
                lb_scr[slot, c], hi_scr[slot, c], lo_scr[slot, c], kept_scr[slot, c] = logits_terms(q, j, h, diag)

    def stage_two(i, slot):
        accs, rems = [], []
        for h in range(hp):
            c = 2 * h
            kept_d, kept_p = kept_scr[slot, c], kept_scr[slot, c + 1]
            logw_d = lb_scr[slot, c] + suffix_sums(hi_scr[slot, c], lo_scr[slot, c])
            logw_p = lb_scr[slot, c + 1] + suffix_sums(hi_scr[slot, c + 1], lo_scr[slot, c + 1])
            w_d = jnp.where(earlier, jnp.exp2(logw_d), 0.0)
            accs.append(pv(w_d, i, h) + pv(jnp.exp2(logw_p + kept_d), i - 1, h))
            rems.append(kept_d + kept_p)
        store(i, accs)
        live_ref[i] = any_live(rems).astype(jnp.int32)

    for h in range(hp):
        _, acc, _ = diag_block(0, h)
        o_ref[0:blk, h * hd:(h + 1) * hd] = acc.astype(o_ref.dtype)

    last = nblk - 1
    if last >= 1:
        stage_one(1, 0)
        steps = last - 1

        def pair(t, carry):
            i = 1 + 2 * t
            stage_two(i, 0)
            stage_one(i + 1, 1)
            stage_two(i + 1, 1)
            stage_one(i + 2, 0)
            return carry

        lax.fori_loop(0, steps // 2, pair, 0)
        if steps % 2:
            stage_two(last - 1, 0)
            stage_one(last, 1)
        stage_two(last, steps % 2)

    def older_keys(i, carry):
        @pl.when(live_ref[i] != 0)
        def _():
            accs, rems = [], []
            for h in range(hp):
                q, acc, kept_d = diag_block(i, h)
                logw, kept_p = block_terms(q, i - 1, h, False)
                accs.append(acc + pv(jnp.exp2(logw + kept_d), i - 1, h))
                rems.append(kept_d + kept_p)

            def cond(c):
                j, live, _, _ = c
                return jnp.logical_and(j >= 0, live)

            def body(c):
                j, _, accs, rems = c
                new_accs, new_rems = [], []
                for h in range(hp):
                    logw, kept = block_terms(load(q_ref, i, h), j, h, False)
                    new_accs.append(accs[h] + pv(jnp.exp2(logw + rems[h]), j, h))
                    new_rems.append(rems[h] + kept)
                return j - 1, any_live(new_rems), tuple(new_accs), tuple(new_rems)

            _, _, accs, _ = lax.while_loop(cond, body, (i - 2, True, tuple(accs), tuple(rems)))
            store(i, accs)
        return carry

    lax.fori_loop(2, nblk, older_keys, 0)


def _sb_attention(q, k, v, *, batch, blk, hp):
    T, W = q.shape
    S = T // batch
    heads = W // SB_HEAD_DIM
    tri = (lax.broadcasted_iota(jnp.int32, (blk, blk), 0)
           > lax.broadcasted_iota(jnp.int32, (blk, blk), 1)).astype(BF16)
    spec = pl.BlockSpec((S, hp * SB_HEAD_DIM), lambda b, h: (b, h))
    chains = 2 * hp
    return pl.pallas_call(
        functools.partial(_sb_kernel, blk=blk, nblk=S // blk, hp=hp),
        grid=(batch, heads // hp),
        in_specs=[spec, spec, spec, _const_spec((blk, blk))],
        out_specs=spec,
        out_shape=jax.ShapeDtypeStruct((T, W), BF16),
        scratch_shapes=[pltpu.VMEM((2, chains, blk, blk), F32),
                        pltpu.VMEM((2, chains, blk, blk), BF16),
                        pltpu.VMEM((2, chains, blk, blk), BF16),
                        pltpu.VMEM((2, chains, blk, 1), F32),
                        pltpu.SMEM((S // blk,), jnp.int32)],
        compiler_params=_cparams(("parallel", "parallel")),
        name="sb_attention",
    )(q, k, v, tri)


def _mla_prep_kernel(cq_ref, ckv_ref, kr_ref, cs_ref, sn_ref, qn_ref, kvn_ref,
                     wq_ref, wqs_ref, wk_ref, wv_ref, q_ref, k_ref, v_ref, *, scale):
    cq = _rms(cq_ref[...], qn_ref[...]).astype(BF16)
    ckv = _rms(ckv_ref[...], kvn_ref[...]).astype(BF16)
    cs = cs_ref[...]
    sn = sn_ref[...]
    kr = kr_ref[...]
    v_ref[...] = jnp.dot(ckv, wv_ref[...], preferred_element_type=F32).astype(BF16)
    for h in range(MLA_HEADS):
        sl = slice(h * MLA_QK_PAD, (h + 1) * MLA_QK_PAD)
        qh = jnp.dot(cq, wq_ref[:, sl], preferred_element_type=F32)
        qh_sw = jnp.dot(cq, wqs_ref[:, sl], preferred_element_type=F32)
        q_ref[:, sl] = ((qh * cs + qh_sw * sn) * scale).astype(BF16)
        kh = jnp.dot(ckv, wk_ref[:, sl], preferred_element_type=F32)
        k_ref[:, sl] = (kh + kr).astype(BF16)


def _mla_prep(cq, ckv, kr, cs, sn, qn, kvn, wq, wqs, wk, wv, l, *, tm):
    T, lora = cq.shape
    qk_w = MLA_HEADS * MLA_QK_PAD
    v_w = MLA_HEADS * MLA_V
    row = lambda width: pl.BlockSpec((tm, width), lambda i: (i, 0))
    return pl.pallas_call(
        functools.partial(_mla_prep_kernel, scale=(MLA_NOPE + MLA_ROPE) ** -0.5 * math.log2(math.e)),
        grid=(T // tm,),
        in_specs=[row(lora), row(lora), row(MLA_QK_PAD), row(MLA_QK_PAD), row(MLA_QK_PAD),
                  _layer_spec((1, lora), l), _layer_spec((1, lora), l),
                  _layer_spec((lora, qk_w), l), _layer_spec((lora, qk_w), l),
                  _layer_spec((lora, qk_w), l), _layer_spec((lora, v_w), l)],
        out_specs=[row(qk_w), row(qk_w), row(v_w)],
        out_shape=[jax.ShapeDtypeStruct((T, qk_w), BF16),
                   jax.ShapeDtypeStruct((T, qk_w), BF16),
                   jax.ShapeDtypeStruct((T, v_w), BF16)],
        compiler_params=_cparams(("parallel",)),
        name="mla_prep",
    )(cq, ckv, kr, cs, sn, qn, kvn, wq, wqs, wk, wv)


def _mla_kernel(q_ref, k_ref, v_ref, o_ref, *, blk, nblk):
    rowc = lax.broadcasted_iota(jnp.int32, (blk, blk), 0) // CHUNK
    colc = lax.broadcasted_iota(jnp.int32, (blk, blk), 1) // CHUNK
    visible = colc <= rowc
    lanes = MLA_V

    def fold(a, op):
        parts = [a[:, c * lanes:(c + 1) * lanes] for c in range(a.shape[1] // lanes)]
        while len(parts) > 1:
            nxt = [op(parts[n], parts[n + 1]) for n in range(0, len(parts) - 1, 2)]
            parts = nxt + parts[len(parts) - len(parts) % 2:]
        return parts[0]

    for i in range(nblk):
        rows = slice(i * blk, (i + 1) * blk)
        q = q_ref[rows, :]
        s_diag = jnp.where(visible, lax.dot_general(q, k_ref[rows, :], _DN_NT, preferred_element_type=F32),
                           -jnp.inf)
        m_lane = fold(s_diag, jnp.maximum)
        if i > 0:
            s_past = lax.dot_general(q, k_ref[0:i * blk, :], _DN_NT, preferred_element_type=F32)
            m_lane = jnp.maximum(m_lane, fold(s_past, jnp.maximum))
        m = jnp.max(m_lane, axis=1, keepdims=True)
        p = jnp.exp2(s_diag - m)
        l_lane = fold(p, jnp.add)
        acc = jnp.dot(p.astype(BF16), v_ref[rows, :], preferred_element_type=F32)
        if i > 0:
            p = jnp.exp2(s_past - m)
            l_lane = l_lane + fold(p, jnp.add)
            acc = acc + jnp.dot(p.astype(BF16), v_ref[0:i * blk, :], preferred_element_type=F32)
        l = jnp.sum(l_lane, axis=1, keepdims=True)
        o_ref[rows, :] = (acc / l).astype(o_ref.dtype)


def _mla_attention(q, k, v, *, batch, blk):
    T = q.shape[0]
    S = T // batch
    qk_spec = pl.BlockSpec((S, MLA_QK_PAD), lambda b, h: (b, h))
    v_spec = pl.BlockSpec((S, MLA_V), lambda b, h: (b, h))
    return pl.pallas_call(
        functools.partial(_mla_kernel, blk=blk, nblk=S // blk),
        grid=(batch, MLA_HEADS),
        in_specs=[qk_spec, qk_spec, v_spec],
        out_specs=v_spec,
        out_shape=jax.ShapeDtypeStruct((T, MLA_HEADS * MLA_V), BF16),
        compiler_params=_cparams(("parallel", "parallel")),
        name="mla_attention",
    )(q, k, v)


def _merge_kernel(x_ref, ya_ref, yb_ref, yc_ref, gate_ref, wa_ref, wb_ref, wc_ref, wo_ref, o_ref, *, d):
    merged = gate_ref[:, 0:d] * jnp.dot(ya_ref[...], wa_ref[...], preferred_element_type=F32)
    merged = merged + gate_ref[:, d:2 * d] * jnp.dot(yb_ref[...], wb_ref[...], preferred_element_type=F32)
    merged = merged + gate_ref[:, 2 * d:3 * d] * jnp.dot(yc_ref[...], wc_ref[...], preferred_element_type=F32)
    o_ref[...] = x_ref[...] + jnp.dot(merged.astype(BF16), wo_ref[...], preferred_element_type=F32)


def _merge(x, ya, yb, yc, gates, wa, wb, wc, wo, l, *, tm):
    T, D = x.shape
    row = lambda width: pl.BlockSpec((tm, width), lambda i: (i, 0))
    wspec = _layer_spec((D, D), l)
    return pl.pallas_call(
        functools.partial(_merge_kernel, d=D),
        grid=(T // tm,),
        in_specs=[row(D), row(D), row(D), row(D), row(3 * D), wspec, wspec, wspec, wspec],
        out_specs=row(D),
        out_shape=jax.ShapeDtypeStruct((T, D), F32),
        compiler_params=_cparams(("parallel",)),
        name="merge",
    )(x, ya, yb, yc, gates, wa, wb, wc, wo)


def _prep_w_in(w_in, d, lora):
    w = w_in.astype(BF16)
    kr_lo = 5 * d + 2 * lora
    kr_hi = kr_lo + MLA_ROPE
    half = MLA_ROPE // 2
    kr = w[:, :, kr_lo:kr_hi]
    lead = w.shape[:2]
    z_nope = jnp.zeros(lead + (MLA_NOPE,), BF16)
    z_tail = jnp.zeros(lead + (MLA_QK_PAD - MLA_NOPE - MLA_ROPE,), BF16)
    kr_pads = jnp.concatenate([z_nope, kr[..., :half], kr[..., half:], z_tail,
                               z_nope, kr[..., half:], kr[..., :half], z_tail], axis=-1)
    return w, w[:, :, kr_hi:], kr_pads


def _prep_w_uq(w_uq):
    L, lora, _ = w_uq.shape
    half = MLA_ROPE // 2
    w = w_uq.reshape(L, lora, MLA_HEADS, MLA_NOPE + MLA_ROPE)
    nope, r1, r2 = w[..., :MLA_NOPE], w[..., MLA_NOPE:MLA_NOPE + half], w[..., MLA_NOPE + half:]
    z_tail = jnp.zeros(w.shape[:3] + (MLA_QK_PAD - MLA_NOPE - MLA_ROPE,), w.dtype)
    plain = jnp.concatenate([nope, r1, r2, z_tail], axis=-1)
    swapped = jnp.concatenate([jnp.zeros_like(nope), r2, r1, z_tail], axis=-1)
    shape = (L, lora, MLA_HEADS * MLA_QK_PAD)
    return plain.reshape(shape).astype(BF16), swapped.reshape(shape).astype(BF16)


def _prep_w_ukv(w_ukv):
    L, lora, _ = w_ukv.shape
    w = w_ukv.reshape(L, lora, MLA_HEADS, MLA_NOPE + MLA_V)
    k_nope, v = w[..., :MLA_NOPE], w[..., MLA_NOPE:]
    z = jnp.zeros(w.shape[:3] + (MLA_QK_PAD - MLA_NOPE,), w.dtype)
    wk = jnp.concatenate([k_nope, z], axis=-1).reshape(L, lora, MLA_HEADS * MLA_QK_PAD)
    wv = v.reshape(L, lora, MLA_HEADS * MLA_V)
    return wk.astype(BF16), wv.astype(BF16)


def _prep_rg_gate(w):
    L, nb, bw, _ = w.shape
    per = RG_GROUP // bw
    w = w.reshape(L, nb // per, per, bw, bw)
    eye = jnp.eye(per, dtype=w.dtype)
    tiles = jnp.einsum('lgpjk,pq->lgpjqk', w, eye)
    return tiles.reshape(L, nb // per, RG_GROUP, RG_GROUP).astype(BF16)


def kernel(x, positions, ffn1_norm, ffn1_w_gate_up, ffn1_w_down, mix_norm, w_in, conv_w, conv_b, rg_w_a, rg_b_a, rg_w_x, rg_b_x, rg_lambda, mla_q_norm, mla_w_uq, mla_kv_norm, mla_w_ukv, w_branch_a, w_branch_b, w_branch_c, w_out, ffn2_norm, ffn2_w_gate_up, ffn2_w_down, final_norm):
    B, S, D = x.shape
    depth = w_in.shape[0]
    lora = mla_q_norm.shape[1]
    T = B * S
    tm = min(512, T)
    tm_in = min(256, T)
    ts = min(256, S)
    blk = min(ATTN_BLOCK, S)
    sb_blk = min(SB_BLOCK, S)

    ffn1_wgu = ffn1_w_gate_up.astype(BF16)
    ffn1_wd = ffn1_w_down.astype(BF16)
    ffn2_wgu = ffn2_w_gate_up.astype(BF16)
    ffn2_wd = ffn2_w_down.astype(BF16)
    w_in_b, w_gate, w_kr = _prep_w_in(w_in, D, lora)
    wq, wq_sw = _prep_w_uq(mla_w_uq)
    wk, wv = _prep_w_ukv(mla_w_ukv)
    rg_wa = _prep_rg_gate(rg_w_a)
    rg_wx = _prep_rg_gate(rg_w_x)
    wba = w_branch_a.astype(BF16)
    wbb = w_branch_b.astype(BF16)
    wbc = w_branch_c.astype(BF16)
    wo = w_out.astype(BF16)

    cs, sn = _rope_tables(positions, tm)
    xf = x.reshape(T, D)
    rows = lambda a: a[:, None, :]
    ffn1_g, mix_g, ffn2_g = rows(ffn1_norm), rows(mix_norm), rows(ffn2_norm)
    cb, ba, bx, lam = rows(conv_b), rows(rg_b_a), rows(rg_b_x), rows(rg_lambda)
    qn, kvn = rows(mla_q_norm), rows(mla_kv_norm)

    for l in range(depth):
        xf = _ffn(xf, ffn1_g, ffn1_wgu, ffn1_wd, l, tm=tm)
        rg_x, rg_g, sb_q, sb_k, sb_v, c_q, c_kv, gates, kr = _inproj(
            xf, mix_g, w_in_b, w_gate, w_kr, cs, sn, l, tm=tm_in, lora=lora)
        y_a = _rglru(rg_x, rg_g, conv_w, cb, rg_wa, ba, rg_wx, bx, lam, l, batch=B, ts=ts)
        y_b = _sb_attention(sb_q, sb_k, sb_v, batch=B, blk=sb_blk, hp=SB_HEADS_PER_STEP)
        mq, mk, mv = _mla_prep(c_q, c_kv, kr, cs, sn, qn, kvn, wq, wq_sw, wk, wv, l, tm=tm)
        y_c = _mla_attention(mq, mk, mv, batch=B, blk=blk)
        xf = _merge(xf, y_a, y_b, y_c, gates, wba, wbb, wbc, wo, l, tm=tm)
        xf = _ffn(xf, ffn2_g, ffn2_wgu, ffn2_wd, l,
                  final_norm[None, :] if l == depth - 1 else None, tm=tm)

    return xf.reshape(B, S, D)
```

```python
import functools
import math

import jax
import jax.numpy as jnp
from jax import lax
from jax.experimental import pallas as pl
from jax.experimental.pallas import tpu as pltpu

F32 = jnp.float32
BF16 = jnp.bfloat16

NORM_EPS = 1e-6
CHUNK = 64
RG_C = 8.0
CONV_W = 4
SB_HEAD_DIM = 128
MLA_HEADS = 8
MLA_NOPE = 128
MLA_ROPE = 64
MLA_V = 128
MLA_QK_PAD = 256
MLA_ROPE_PAD = MLA_QK_PAD - MLA_NOPE
ROPE_THETA = 10000.0

V7X_VMEM_LIMIT_BYTES = 56 * 1024 * 1024
RG_GROUP = 256
SUBLANES = 8
LOG2_E = math.log2(math.e)
SB_DEAD_LOG2 = -105.0 * LOG2_E
SB_HEADS_PER_STEP = 4
SB_BLOCK = 256
ATTN_BLOCK = 512

_DN_NT = (((1,), (1,)), ((), ()))


def _cparams(sem):
    return pltpu.CompilerParams(dimension_semantics=sem,
                                vmem_limit_bytes=V7X_VMEM_LIMIT_BYTES)


def _const_spec(shape):
    nd = len(shape)
    return pl.BlockSpec(shape, lambda *_: (0,) * nd, pipeline_mode=pl.Buffered(1))


def _layer_spec(shape, l):
    nd = len(shape)
    return pl.BlockSpec((None,) + tuple(shape), lambda *_: (l,) + (0,) * nd,
                        pipeline_mode=pl.Buffered(1))


def _rms(x, g):
    return x * lax.rsqrt(jnp.mean(x * x, axis=-1, keepdims=True) + NORM_EPS) * g


def _rope_kernel(pos_ref, inv_ref, cs_ref, sn_ref):
    ang = pos_ref[...] * inv_ref[...]
    lane = lax.broadcasted_iota(jnp.int32, ang.shape, 1)
    half = MLA_ROPE // 2
    c = jnp.cos(ang)
    s = jnp.sin(ang)
    in_rope = lane < MLA_ROPE
    cs_ref[...] = jnp.where(in_rope, c, 0.0)
    sn_ref[...] = jnp.where(in_rope, jnp.where(lane < half, -s, s), 0.0)


def _rope_tables(positions, tm):
    T = positions.size
    pos = positions.reshape(T, 1).astype(F32)
    inv = ROPE_THETA ** (-jnp.arange(0, MLA_ROPE, 2, dtype=F32) / MLA_ROPE)
    inv_pad = jnp.concatenate([inv, inv, jnp.zeros((MLA_ROPE_PAD - MLA_ROPE,), F32)])[None, :]
    return pl.pallas_call(
        _rope_kernel,
        grid=(T // tm,),
        in_specs=[pl.BlockSpec((tm, 1), lambda i: (i, 0)),
                  pl.BlockSpec((1, MLA_ROPE_PAD), lambda i: (0, 0))],
        out_specs=[pl.BlockSpec((tm, MLA_ROPE_PAD), lambda i: (i, 0))] * 2,
        out_shape=[jax.ShapeDtypeStruct((T, MLA_ROPE_PAD), F32)] * 2,
        compiler_params=_cparams(("parallel",)),
        name="rope_tables",
    )(pos, inv_pad)


def _ffn_kernel(x_ref, g_ref, wgu_ref, wd_ref, *rest, d_ff, n_chunk, final):
    if final:
        fg_ref, o_ref, a_ref = rest
    else:
        o_ref, a_ref = rest
    x = x_ref[...]
    h = _rms(x, g_ref[...]).astype(BF16)
    tf = d_ff // n_chunk
    for c in range(n_chunk):
        gate = jnp.dot(h, wgu_ref[:, c * tf:(c + 1) * tf], preferred_element_type=F32)
        up = jnp.dot(h, wgu_ref[:, d_ff + c * tf:d_ff + (c + 1) * tf], preferred_element_type=F32)
        a_ref[:, c * tf:(c + 1) * tf] = (gate * jax.nn.sigmoid(gate) * up).astype(BF16)
    y = x + 0.5 * jnp.dot(a_ref[...], wd_ref[...], preferred_element_type=F32)
    if final:
        y = _rms(y, fg_ref[...])
    o_ref[...] = y


def _ffn(x, g, wgu, wd, l, final_g=None, *, tm):
    T, D = x.shape
    d_ff = wd.shape[1]
    n_chunk = 2 if (d_ff // 2) % 128 == 0 else 1
    final = final_g is not None
    in_specs = [pl.BlockSpec((tm, D), lambda i: (i, 0)),
                _layer_spec((1, D), l),
                _layer_spec((D, 2 * d_ff), l),
                _layer_spec((d_ff, D), l)]
    args = [x, g, wgu, wd]
    if final:
        in_specs.append(_const_spec((1, D)))
        args.append(final_g)
    return pl.pallas_call(
        functools.partial(_ffn_kernel, d_ff=d_ff, n_chunk=n_chunk, final=final),
        grid=(T // tm,),
        in_specs=in_specs,
        out_specs=pl.BlockSpec((tm, D), lambda i: (i, 0)),
        out_shape=jax.ShapeDtypeStruct((T, D), F32),
        scratch_shapes=[pltpu.VMEM((tm, d_ff), BF16)],
        compiler_params=_cparams(("parallel",)),
        name="ffn_final" if final else "ffn",
    )(*args)


def _inproj_kernel(x_ref, g_ref, w_ref, wg_ref, wkr_ref, cs_ref, sn_ref,
                   rgx_ref, rgg_ref, q_ref, k_ref, v_ref, cq_ref, ckv_ref, gate_ref, kr_ref,
                   *, d, lora, sb_scale):
    h = _rms(x_ref[...], g_ref[...]).astype(BF16)

    def proj(ref, lo, width):
        return jnp.dot(h, ref[:, lo:lo + width], preferred_element_type=F32)

    o = 0
    rgx_ref[...] = proj(w_ref, o, d); o += d
    rgg_ref[...] = proj(w_ref, o, d); o += d
    q_ref[...] = (proj(w_ref, o, d) * sb_scale).astype(BF16); o += d
    k_ref[...] = proj(w_ref, o, d).astype(BF16); o += d
    v_ref[...] = proj(w_ref, o, d).astype(BF16); o += d
    cq_ref[...] = proj(w_ref, o, lora); o += lora
    ckv_ref[...] = proj(w_ref, o, lora)
    gate_ref[...] = jax.nn.sigmoid(proj(wg_ref, 0, 3 * d))
    kr = proj(wkr_ref, 0, MLA_ROPE_PAD)
    kr_sw = proj(wkr_ref, MLA_ROPE_PAD, MLA_ROPE_PAD)
    kr_ref[...] = kr * cs_ref[...] + kr_sw * sn_ref[...]


def _inproj(x, g, w, w_gate, w_kr, cs, sn, l, *, tm, lora):
    T, D = x.shape
    n_lead = 5 * D + 2 * lora
    row = lambda width: pl.BlockSpec((tm, width), lambda i: (i, 0))
    sds = lambda width, dt: jax.ShapeDtypeStruct((T, width), dt)
    return pl.pallas_call(
        functools.partial(_inproj_kernel, d=D, lora=lora, sb_scale=SB_HEAD_DIM ** -0.5 * LOG2_E),
        grid=(T // tm,),
        in_specs=[row(D), _layer_spec((1, D), l), _layer_spec((D, n_lead), l),
                  _layer_spec((D, 3 * D), l), _layer_spec((D, 2 * MLA_ROPE_PAD), l),
                  row(MLA_ROPE_PAD), row(MLA_ROPE_PAD)],
        out_specs=[row(D), row(D), row(D), row(D), row(D), row(lora), row(lora),
                   row(3 * D), row(MLA_ROPE_PAD)],
        out_shape=[sds(D, F32), sds(D, F32), sds(D, BF16), sds(D, BF16), sds(D, BF16),
                   sds(lora, F32), sds(lora, F32), sds(3 * D, F32), sds(MLA_ROPE_PAD, F32)],
        compiler_params=_cparams(("parallel",)),
        name="in_proj",
    )(x, g, w, w_gate, w_kr, cs, sn)


def _rglru_kernel(x_ref, gate_ref, cw_ref, cb_ref, wa_ref, ba_ref, wx_ref, bx_ref, lam_ref,
                  o_ref, xbuf, hprev, *, ts, n_group):
    s = pl.program_id(1)
    halo = SUBLANES

    @pl.when(s == 0)
    def _():
        xbuf[0:halo, :] = jnp.zeros((halo, xbuf.shape[1]), F32)
        hprev[...] = jnp.zeros(hprev.shape, F32)

    xbuf[halo:halo + ts, :] = x_ref[...]
    n_sub = ts // SUBLANES
    sub = lax.broadcasted_iota(jnp.int32, (n_sub, SUBLANES, RG_GROUP), 1)

    for gi in range(n_group):
        sl = slice(gi * RG_GROUP, (gi + 1) * RG_GROUP)
        u = cb_ref[:, sl] + cw_ref[CONV_W - 1:CONV_W, sl] * xbuf[halo:halo + ts, sl]
        for back in range(1, CONV_W):
            tap = CONV_W - 1 - back
            u = u + cw_ref[tap:tap + 1, sl] * xbuf[halo - back:halo - back + ts, sl]
        ub = u.astype(BF16)
        r = jax.nn.sigmoid(jnp.dot(ub, wa_ref[gi], preferred_element_type=F32) + ba_ref[:, sl])
        i_gate = jax.nn.sigmoid(jnp.dot(ub, wx_ref[gi], preferred_element_type=F32) + bx_ref[:, sl])
        lam = lam_ref[:, sl]
        neg_softplus = -(jnp.maximum(-lam, 0.0) + jnp.log1p(jnp.exp(-jnp.abs(lam))))
        log_a = (RG_C * neg_softplus) * r
        a = jnp.exp(log_a)
        b = jnp.sqrt(-jnp.tanh(log_a) * (a * a + 1.0)) * (i_gate * u)
        a = a.reshape(n_sub, SUBLANES, RG_GROUP)
        b = b.reshape(n_sub, SUBLANES, RG_GROUP)
        d = 1
        while d < SUBLANES:
            m = sub >= d
            a_s = pltpu.roll(a, d, 1)
            b_s = pltpu.roll(b, d, 1)
            b = jnp.where(m, a * b_s + b, b)
            a = jnp.where(m, a * a_s, a)
            d *= 2
        state = hprev[:, sl]
        groups = []
        for g in range(n_sub):
            hg = a[g] * state + b[g]
            state = hg[SUBLANES - 1:SUBLANES, :]
            groups.append(hg)
        hs = jnp.concatenate(groups, axis=0)
        hprev[:, sl] = state
        o_ref[:, sl] = (hs * jax.nn.gelu(gate_ref[:, sl], approximate=True)).astype(o_ref.dtype)

    xbuf[0:halo, :] = x_ref[ts - halo:ts, :]


def _rglru(rg_x, rg_g, cw, cb, wa, ba, wx, bx, lam, l, *, batch, ts):
    T, C = rg_x.shape
    S = T // batch
    ns = S // ts
    n_group = C // RG_GROUP
    row = pl.BlockSpec((ts, C), lambda b, s: (b * ns + s, 0))
    return pl.pallas_call(
        functools.partial(_rglru_kernel, ts=ts, n_group=n_group),
        grid=(batch, ns),
        in_specs=[row, row,
                  _layer_spec((CONV_W, C), l), _layer_spec((1, C), l),
                  _layer_spec((n_group, RG_GROUP, RG_GROUP), l), _layer_spec((1, C), l),
                  _layer_spec((n_group, RG_GROUP, RG_GROUP), l), _layer_spec((1, C), l),
                  _layer_spec((1, C), l)],
        out_specs=row,
        out_shape=jax.ShapeDtypeStruct((T, C), BF16),
        scratch_shapes=[pltpu.VMEM((ts + SUBLANES, C), F32), pltpu.VMEM((1, C), F32)],
        compiler_params=_cparams(("parallel", "arbitrary")),
        name="rglru",
    )(rg_x, rg_g, cw, cb, wa, ba, wx, bx, lam)


def _sb_kernel(q_ref, k_ref, v_ref, tri_ref, o_ref, *, blk, nblk, hp):
    rowi = lax.broadcasted_iota(jnp.int32, (blk, blk), 0)
    coli = lax.broadcasted_iota(jnp.int32, (blk, blk), 1)
    earlier = coli < rowi
    hd = SB_HEAD_DIM

    def load(ref, j, h):
        r0 = j * blk if isinstance(j, int) else pl.multiple_of(j * blk, blk)
        return ref[pl.ds(r0, blk), h * hd:(h + 1) * hd]

    def block_terms(q, j, h, diag):
        z = lax.dot_general(q, load(k_ref, j, h), _DN_NT, preferred_element_type=F32)
        tail = jnp.log(1.0 + jnp.exp2(-jnp.abs(z))) * LOG2_E
        log_beta = jnp.minimum(z, 0.0) - tail
        log_keep = log_beta - z
        if diag:
            log_keep = jnp.where(earlier, log_keep, 0.0)
        hi = log_keep.astype(BF16)
        lo = (log_keep - hi.astype(F32)).astype(BF16)
        tri = tri_ref[...]
        between = (jnp.dot(hi, tri, preferred_element_type=F32)
                   + jnp.dot(lo, tri, preferred_element_type=F32))
        return log_beta + between, jnp.sum(log_keep, axis=1, keepdims=True)

    def pv(w, j, h):
        return jnp.dot(w.astype(BF16), load(v_ref, j, h), preferred_element_type=F32)

    def diag_block(i, h):
        q = load(q_ref, i, h)
        logw, kept = block_terms(q, i, h, True)
        w = jnp.where(earlier, jnp.exp2(logw), 0.0)
        return q, pv(w, i, h), kept

    def any_live(rems):
        top = rems[0]
        for r in rems[1:]:
            top = jnp.maximum(top, r)
        return jnp.max(top) > SB_DEAD_LOG2

    for h in range(hp):
        _, acc, _ = diag_block(0, h)
        o_ref[0:blk, h * hd:(h + 1) * hd] = acc.astype(o_ref.dtype)

    def q_body(i, carry):
        accs, rems = [], []
        for h in range(hp):
            q, acc, kept_d = diag_block(i, h)
            logw, kept_p = block_terms(q, i - 1, h, False)
            accs.append(acc + pv(jnp.exp2(logw + kept_d), i - 1, h))
            rems.append(kept_d + kept_p)

        def cond(c):
            j, live, _, _ = c
            return jnp.logical_and(j >= 0, live)

        def body(c):
            j, _, accs, rems = c
            new_accs, new_rems = [], []
            for h in range(hp):
                logw, kept = block_terms(load(q_ref, i, h), j, h, False)
                new_accs.append(accs[h] + pv(jnp.exp2(logw + rems[h]), j, h))
                new_rems.append(rems[h] + kept)
            return j - 1, any_live(new_rems), tuple(new_accs), tuple(new_rems)

        _, _, accs, _ = lax.while_loop(cond, body, (i - 2, any_live(rems), tuple(accs), tuple(rems)))
        q0 = pl.multiple_of(i * blk, blk)
        for h in range(hp):
            o_ref[pl.ds(q0, blk), h * hd:(h + 1) * hd] = accs[h].astype(o_ref.dtype)
        return carry

    lax.fori_loop(1, nblk, q_body, 0)


def _sb_attention(q, k, v, *, batch, blk, hp):
    T, W = q.shape
    S = T // batch
    heads = W // SB_HEAD_DIM
    tri = (lax.broadcasted_iota(jnp.int32, (blk, blk), 0)
           > lax.broadcasted_iota(jnp.int32, (blk, blk), 1)).astype(BF16)
    spec = pl.BlockSpec((S, hp * SB_HEAD_DIM), lambda b, h: (b, h))
    return pl.pallas_call(
        functools.partial(_sb_kernel, blk=blk, nblk=S // blk, hp=hp),
        grid=(batch, heads // hp),
        in_specs=[spec, spec, spec, _const_spec((blk, blk))],
        out_specs=spec,
        out_shape=jax.ShapeDtypeStruct((T, W), BF16),
        compiler_params=_cparams(("parallel", "parallel")),
        name="sb_attention",
    )(q, k, v, tri)


def _mla_prep_kernel(cq_ref, ckv_ref, kr_ref, cs_ref, sn_ref, qn_ref, kvn_ref,
                     wq_ref, wqs_ref, wk_ref, wv_ref, q_ref, k_ref, v_ref, *, scale):
    cq = _rms(cq_ref[...], qn_ref[...]).astype(BF16)
    ckv = _rms(ckv_ref[...], kvn_ref[...]).astype(BF16)
    cs = cs_ref[...]
    sn = sn_ref[...]
    kr = kr_ref[...].astype(BF16)
    v_ref[...] = jnp.dot(ckv, wv_ref[...], preferred_element_type=F32).astype(BF16)
    for h in range(MLA_HEADS):
        nope = slice(h * MLA_QK_PAD, h * MLA_QK_PAD + MLA_NOPE)
        rope = slice(h * MLA_QK_PAD + MLA_NOPE, (h + 1) * MLA_QK_PAD)
        tile = slice(h * MLA_ROPE_PAD, (h + 1) * MLA_ROPE_PAD)
        q_ref[:, nope] = (jnp.dot(cq, wq_ref[:, nope], preferred_element_type=F32) * scale).astype(BF16)
        qr = jnp.dot(cq, wq_ref[:, rope], preferred_element_type=F32)
        qr_sw = jnp.dot(cq, wqs_ref[:, tile], preferred_element_type=F32)
        q_ref[:, rope] = ((qr * cs + qr_sw * sn) * scale).astype(BF16)
        k_ref[:, nope] = jnp.dot(ckv, wk_ref[:, h * MLA_NOPE:(h + 1) * MLA_NOPE],
                                 preferred_element_type=F32).astype(BF16)
        k_ref[:, rope] = kr


def _mla_prep(cq, ckv, kr, cs, sn, qn, kvn, wq, wqs, wk, wv, l, *, tm):
    T, lora = cq.shape
    qk_w = MLA_HEADS * MLA_QK_PAD
    v_w = MLA_HEADS * MLA_V
    row = lambda width: pl.BlockSpec((tm, width), lambda i: (i, 0))
    return pl.pallas_call(
        functools.partial(_mla_prep_kernel, scale=(MLA_NOPE + MLA_ROPE) ** -0.5 * math.log2(math.e)),
        grid=(T // tm,),
        in_specs=[row(lora), row(lora), row(MLA_ROPE_PAD), row(MLA_ROPE_PAD), row(MLA_ROPE_PAD),
                  _layer_spec((1, lora), l), _layer_spec((1, lora), l),
                  _layer_spec((lora, qk_w), l), _layer_spec((lora, MLA_HEADS * MLA_ROPE_PAD), l),
                  _layer_spec((lora, MLA_HEADS * MLA_NOPE), l), _layer_spec((lora, v_w), l)],
        out_specs=[row(qk_w), row(qk_w), row(v_w)],
        out_shape=[jax.ShapeDtypeStruct((T, qk_w), BF16),
                   jax.ShapeDtypeStruct((T, qk_w), BF16),
                   jax.ShapeDtypeStruct((T, v_w), BF16)],
        compiler_params=_cparams(("parallel",)),
        name="mla_prep",
    )(cq, ckv, kr, cs, sn, qn, kvn, wq, wqs, wk, wv)


def _mla_kernel(q_ref, k_ref, v_ref, o_ref, *, blk, nblk):
    rowc = lax.broadcasted_iota(jnp.int32, (blk, blk), 0) // CHUNK
    colc = lax.broadcasted_iota(jnp.int32, (blk, blk), 1) // CHUNK
    visible = colc <= rowc
    lanes = MLA_V

    def fold(a, op):
        parts = [a[:, c * lanes:(c + 1) * lanes] for c in range(a.shape[1] // lanes)]
        while len(parts) > 1:
            nxt = [op(parts[n], parts[n + 1]) for n in range(0, len(parts) - 1, 2)]
            parts = nxt + parts[len(parts) - len(parts) % 2:]
        return parts[0]

    for i in range(nblk):
        rows = slice(i * blk, (i + 1) * blk)
        q = q_ref[rows, :]
        s_diag = jnp.where(visible, lax.dot_general(q, k_ref[rows, :], _DN_NT, preferred_element_type=F32),
                           -jnp.inf)
        m_lane = fold(s_diag, jnp.maximum)
        if i > 0:
            s_past = lax.dot_general(q, k_ref[0:i * blk, :], _DN_NT, preferred_element_type=F32)
            m_lane = jnp.maximum(m_lane, fold(s_past, jnp.maximum))
        m = jnp.max(m_lane, axis=1, keepdims=True)
        p = jnp.exp2(s_diag - m)
        l_lane = fold(p, jnp.add)
        acc = jnp.dot(p.astype(BF16), v_ref[rows, :], preferred_element_type=F32)
        if i > 0:
            p = jnp.exp2(s_past - m)
            l_lane = l_lane + fold(p, jnp.add)
            acc = acc + jnp.dot(p.astype(BF16), v_ref[0:i * blk, :], preferred_element_type=F32)
        l = jnp.sum(l_lane, axis=1, keepdims=True)
        o_ref[rows, :] = (acc / l).astype(o_ref.dtype)


def _mla_attention(q, k, v, *, batch, blk):
    T = q.shape[0]
    S = T // batch
    qk_spec = pl.BlockSpec((S, MLA_QK_PAD), lambda b, h: (b, h))
    v_spec = pl.BlockSpec((S, MLA_V), lambda b, h: (b, h))
    return pl.pallas_call(
        functools.partial(_mla_kernel, blk=blk, nblk=S // blk),
        grid=(batch, MLA_HEADS),
        in_specs=[qk_spec, qk_spec, v_spec],
        out_specs=v_spec,
        out_shape=jax.ShapeDtypeStruct((T, MLA_HEADS * MLA_V), BF16),
        compiler_params=_cparams(("parallel", "parallel")),
        name="mla_attention",
    )(q, k, v)


def _merge_kernel(x_ref, ya_ref, yb_ref, yc_ref, gate_ref, wa_ref, wb_ref, wc_ref, wo_ref, o_ref, *, d):
    merged = gate_ref[:, 0:d] * jnp.dot(ya_ref[...], wa_ref[...], preferred_element_type=F32)
    merged = merged + gate_ref[:, d:2 * d] * jnp.dot(yb_ref[...], wb_ref[...], preferred_element_type=F32)
    merged = merged + gate_ref[:, 2 * d:3 * d] * jnp.dot(yc_ref[...], wc_ref[...], preferred_element_type=F32)
    o_ref[...] = x_ref[...] + jnp.dot(merged.astype(BF16), wo_ref[...], preferred_element_type=F32)


def _merge(x, ya, yb, yc, gates, wa, wb, wc, wo, l, *, tm):
    T, D = x.shape
    row = lambda width: pl.BlockSpec((tm, width), lambda i: (i, 0))
    wspec = _layer_spec((D, D), l)
    return pl.pallas_call(
        functools.partial(_merge_kernel, d=D),
        grid=(T // tm,),
        in_specs=[row(D), row(D), row(D), row(D), row(3 * D), wspec, wspec, wspec, wspec],
        out_specs=row(D),
        out_shape=jax.ShapeDtypeStruct((T, D), F32),
        compiler_params=_cparams(("parallel",)),
        name="merge",
    )(x, ya, yb, yc, gates, wa, wb, wc, wo)


def _prep_w_in(w_in, d, lora):
    w = w_in.astype(BF16)
    kr_lo = 5 * d + 2 * lora
    kr_hi = kr_lo + MLA_ROPE
    half = MLA_ROPE // 2
    kr = w[:, :, kr_lo:kr_hi]
    lead = w.shape[:2]
    z_tail = jnp.zeros(lead + (MLA_ROPE_PAD - MLA_ROPE,), BF16)
    kr_pads = jnp.concatenate([kr[..., :half], kr[..., half:], z_tail,
                               kr[..., half:], kr[..., :half], z_tail], axis=-1)
    return w, w[:, :, kr_hi:], kr_pads


def _prep_w_uq(w_uq):
    L, lora, _ = w_uq.shape
    half = MLA_ROPE // 2
    w = w_uq.reshape(L, lora, MLA_HEADS, MLA_NOPE + MLA_ROPE)
    nope, r1, r2 = w[..., :MLA_NOPE], w[..., MLA_NOPE:MLA_NOPE + half], w[..., MLA_NOPE + half:]
    z_tail = jnp.zeros(w.shape[:3] + (MLA_QK_PAD - MLA_NOPE - MLA_ROPE,), w.dtype)
    plain = jnp.concatenate([nope, r1, r2, z_tail], axis=-1)
    swapped = jnp.concatenate([r2, r1, z_tail], axis=-1)
    return (plain.reshape(L, lora, MLA_HEADS * MLA_QK_PAD).astype(BF16),
            swapped.reshape(L, lora, MLA_HEADS * MLA_ROPE_PAD).astype(BF16))


def _prep_w_ukv(w_ukv):
    L, lora, _ = w_ukv.shape
    w = w_ukv.reshape(L, lora, MLA_HEADS, MLA_NOPE + MLA_V)
    k_nope, v = w[..., :MLA_NOPE], w[..., MLA_NOPE:]
    wk = k_nope.reshape(L, lora, MLA_HEADS * MLA_NOPE)
    wv = v.reshape(L, lora, MLA_HEADS * MLA_V)
    return wk.astype(BF16), wv.astype(BF16)


def _prep_rg_gate(w):
    L, nb, bw, _ = w.shape
    per = RG_GROUP // bw
    w = w.reshape(L, nb // per, per, bw, bw)
    eye = jnp.eye(per, dtype=w.dtype)
    tiles = jnp.einsum('lgpjk,pq->lgpjqk', w, eye)
    return tiles.reshape(L, nb // per, RG_GROUP, RG_GROUP).astype(BF16)


def kernel(x, positions, ffn1_norm, ffn1_w_gate_up, ffn1_w_down, mix_norm, w_in, conv_w, conv_b, rg_w_a, rg_b_a, rg_w_x, rg_b_x, rg_lambda, mla_q_norm, mla_w_uq, mla_kv_norm, mla_w_ukv, w_branch_a, w_branch_b, w_branch_c, w_out, ffn2_norm, ffn2_w_gate_up, ffn2_w_down, final_norm):
    B, S, D = x.shape
    depth = w_in.shape[0]
    lora = mla_q_norm.shape[1]
    T = B * S
    tm = min(512, T)
    tm_in = min(256, T)
    ts = min(256, S)
    blk = min(ATTN_BLOCK, S)
    sb_blk = min(SB_BLOCK, S)

    ffn1_wgu = ffn1_w_gate_up.astype(BF16)
    ffn1_wd = ffn1_w_down.astype(BF16)
    ffn2_wgu = ffn2_w_gate_up.astype(BF16)
    ffn2_wd = ffn2_w_down.astype(BF16)
    w_in_b, w_gate, w_kr = _prep_w_in(w_in, D, lora)
    wq, wq_sw = _prep_w_uq(mla_w_uq)
    wk, wv = _prep_w_ukv(mla_w_ukv)
    rg_wa = _prep_rg_gate(rg_w_a)
    rg_wx = _prep_rg_gate(rg_w_x)
    wba = w_branch_a.astype(BF16)
    wbb = w_branch_b.astype(BF16)
    wbc = w_branch_c.astype(BF16)
    wo = w_out.astype(BF16)

    cs, sn = _rope_tables(positions, tm)
    xf = x.reshape(T, D)
    rows = lambda a: a[:, None, :]
    ffn1_g, mix_g, ffn2_g = rows(ffn1_norm), rows(mix_norm), rows(ffn2_norm)
    cb, ba, bx, lam = rows(conv_b), rows(rg_b_a), rows(rg_b_x), rows(rg_lambda)
    qn, kvn = rows(mla_q_norm), rows(mla_kv_norm)

    for l in range(depth):
        xf = _ffn(xf, ffn1_g, ffn1_wgu, ffn1_wd, l, tm=tm)
        rg_x, rg_g, sb_q, sb_k, sb_v, c_q, c_kv, gates, kr = _inproj(
            xf, mix_g, w_in_b, w_gate, w_kr, cs, sn, l, tm=tm_in, lora=lora)
        y_a = _rglru(rg_x, rg_g, conv_w, cb, rg_wa, ba, rg_wx, bx, lam, l, batch=B, ts=ts)
        y_b = _sb_attention(sb_q, sb_k, sb_v, batch=B, blk=sb_blk, hp=SB_HEADS_PER_STEP)
        mq, mk, mv = _mla_prep(c_q, c_kv, kr, cs, sn, qn, kvn, wq, wq_sw, wk, wv, l, tm=tm)
        y_c = _mla_attention(mq, mk, mv, batch=B, blk=blk)
        xf = _merge(xf, y_a, y_b, y_c, gates, wba, wbb, wbc, wo, l, tm=tm)
        xf = _ffn(xf, ffn2_g, ffn2_wgu, ffn2_wd, l,
                  final_norm[None, :] if l == depth - 1 else None, tm=tm)

    return xf.reshape(B, S, D)
```

```python
import functools
import math

import jax
import jax.numpy as jnp
from jax import lax
from jax.experimental import pallas as pl
from jax.experimental.pallas import tpu as pltpu

F32 = jnp.float32
BF16 = jnp.bfloat16

NORM_EPS = 1e-6
CHUNK = 64
RG_C = 8.0
CONV_W = 4
SB_HEAD_DIM = 128
MLA_HEADS = 8
MLA_NOPE = 128
MLA_ROPE = 64
MLA_V = 128
MLA_QK_PAD = 256
ROPE_THETA = 10000.0

V7X_VMEM_LIMIT_BYTES = 56 * 1024 * 1024
RG_GROUP = 256
SUBLANES = 8
LOG2_E = math.log2(math.e)
SB_DEAD_LOG2 = -105.0 * LOG2_E
SB_HEADS_PER_STEP = 4
SB_BLOCK = 256
ATTN_BLOCK = 256

_DN_NT = (((1,), (1,)), ((), ()))


def _cparams(sem):
    return pltpu.CompilerParams(dimension_semantics=sem,
                                vmem_limit_bytes=V7X_VMEM_LIMIT_BYTES)


def _const_spec(shape):
    nd = len(shape)
    return pl.BlockSpec(shape, lambda *_: (0,) * nd, pipeline_mode=pl.Buffered(1))


def _layer_spec(shape, l):
    nd = len(shape)
    return pl.BlockSpec((None,) + tuple(shape), lambda *_: (l,) + (0,) * nd,
                        pipeline_mode=pl.Buffered(1))


def _rms(x, g):
    return x * lax.rsqrt(jnp.mean(x * x, axis=-1, keepdims=True) + NORM_EPS) * g


def _rope_kernel(pos_ref, inv_ref, cs_ref, sn_ref):
    ang = pos_ref[...] * inv_ref[...]
    lane = lax.broadcasted_iota(jnp.int32, ang.shape, 1)
    half = MLA_ROPE // 2
    c = jnp.cos(ang)
    s = jnp.sin(ang)
    in_rope = (lane >= MLA_NOPE) & (lane < MLA_NOPE + MLA_ROPE)
    cs_ref[...] = jnp.where(lane < MLA_NOPE, 1.0, jnp.where(in_rope, c, 0.0))
    sn_ref[...] = jnp.where(in_rope, jnp.where(lane < MLA_NOPE + half, -s, s), 0.0)


def _rope_tables(positions, tm):
    T = positions.size
    pos = positions.reshape(T, 1).astype(F32)
    inv = ROPE_THETA ** (-jnp.arange(0, MLA_ROPE, 2, dtype=F32) / MLA_ROPE)
    inv_pad = jnp.concatenate([jnp.zeros((MLA_NOPE,), F32), inv, inv,
                               jnp.zeros((MLA_QK_PAD - MLA_NOPE - MLA_ROPE,), F32)])[None, :]
    return pl.pallas_call(
        _rope_kernel,
        grid=(T // tm,),
        in_specs=[pl.BlockSpec((tm, 1), lambda i: (i, 0)),
                  pl.BlockSpec((1, MLA_QK_PAD), lambda i: (0, 0))],
        out_specs=[pl.BlockSpec((tm, MLA_QK_PAD), lambda i: (i, 0))] * 2,
        out_shape=[jax.ShapeDtypeStruct((T, MLA_QK_PAD), F32)] * 2,
        compiler_params=_cparams(("parallel",)),
        name="rope_tables",
    )(pos, inv_pad)


def _ffn_kernel(x_ref, g_ref, wgu_ref, wd_ref, *rest, d_ff, n_chunk, final):
    if final:
        fg_ref, o_ref, a_ref = rest
    else:
        o_ref, a_ref = rest
    x = x_ref[...]
    h = _rms(x, g_ref[...]).astype(BF16)
    tf = d_ff // n_chunk
    for c in range(n_chunk):
        gate = jnp.dot(h, wgu_ref[:, c * tf:(c + 1) * tf], preferred_element_type=F32)
        up = jnp.dot(h, wgu_ref[:, d_ff + c * tf:d_ff + (c + 1) * tf], preferred_element_type=F32)
        a_ref[:, c * tf:(c + 1) * tf] = (gate * jax.nn.sigmoid(gate) * up).astype(BF16)
    y = x + 0.5 * jnp.dot(a_ref[...], wd_ref[...], preferred_element_type=F32)
    if final:
        y = _rms(y, fg_ref[...])
    o_ref[...] = y


def _ffn(x, g, wgu, wd, l, final_g=None, *, tm):
    T, D = x.shape
    d_ff = wd.shape[1]
    n_chunk = 2 if (d_ff // 2) % 128 == 0 else 1
    final = final_g is not None
    in_specs = [pl.BlockSpec((tm, D), lambda i: (i, 0)),
                _layer_spec((1, D), l),
                _layer_spec((D, 2 * d_ff), l),
                _layer_spec((d_ff, D), l)]
    args = [x, g, wgu, wd]
    if final:
        in_specs.append(_const_spec((1, D)))
        args.append(final_g)
    return pl.pallas_call(
        functools.partial(_ffn_kernel, d_ff=d_ff, n_chunk=n_chunk, final=final),
        grid=(T // tm,),
        in_specs=in_specs,
        out_specs=pl.BlockSpec((tm, D), lambda i: (i, 0)),
        out_shape=jax.ShapeDtypeStruct((T, D), F32),
        scratch_shapes=[pltpu.VMEM((tm, d_ff), BF16)],
        compiler_params=_cparams(("parallel",)),
        name="ffn_final" if final else "ffn",
    )(*args)


def _inproj_kernel(x_ref, g_ref, w_ref, wg_ref, wkr_ref, cs_ref, sn_ref,
                   rgx_ref, rgg_ref, q_ref, k_ref, v_ref, cq_ref, ckv_ref, gate_ref, kr_ref,
                   *, d, lora, sb_scale):
    h = _rms(x_ref[...], g_ref[...]).astype(BF16)

    def proj(ref, lo, width):
        return jnp.dot(h, ref[:, lo:lo + width], preferred_element_type=F32)

    o = 0
    rgx_ref[...] = proj(w_ref, o, d); o += d
    rgg_ref[...] = proj(w_ref, o, d); o += d
    q_ref[...] = (proj(w_ref, o, d) * sb_scale).astype(BF16); o += d
    k_ref[...] = proj(w_ref, o, d).astype(BF16); o += d
    v_ref[...] = proj(w_ref, o, d).astype(BF16); o += d
    cq_ref[...] = proj(w_ref, o, lora); o += lora
    ckv_ref[...] = proj(w_ref, o, lora)
    gate_ref[...] = jax.nn.sigmoid(proj(wg_ref, 0, 3 * d))
    kr = proj(wkr_ref, 0, MLA_QK_PAD)
    kr_sw = proj(wkr_ref, MLA_QK_PAD, MLA_QK_PAD)
    kr_ref[...] = kr * cs_ref[...] + kr_sw * sn_ref[...]


def _inproj(x, g, w, w_gate, w_kr, cs, sn, l, *, tm, lora):
    T, D = x.shape
    n_lead = 5 * D + 2 * lora
    row = lambda width: pl.BlockSpec((tm, width), lambda i: (i, 0))
    sds = lambda width, dt: jax.ShapeDtypeStruct((T, width), dt)
    return pl.pallas_call(
        functools.partial(_inproj_kernel, d=D, lora=lora, sb_scale=SB_HEAD_DIM ** -0.5 * LOG2_E),
        grid=(T // tm,),
        in_specs=[row(D), _layer_spec((1, D), l), _layer_spec((D, n_lead), l),
                  _layer_spec((D, 3 * D), l), _layer_spec((D, 2 * MLA_QK_PAD), l),
                  row(MLA_QK_PAD), row(MLA_QK_PAD)],
        out_specs=[row(D), row(D), row(D), row(D), row(D), row(lora), row(lora),
                   row(3 * D), row(MLA_QK_PAD)],
        out_shape=[sds(D, F32), sds(D, F32), sds(D, BF16), sds(D, BF16), sds(D, BF16),
                   sds(lora, F32), sds(lora, F32), sds(3 * D, F32), sds(MLA_QK_PAD, F32)],
        compiler_params=_cparams(("parallel",)),
        name="in_proj",
    )(x, g, w, w_gate, w_kr, cs, sn)


def _rglru_kernel(x_ref, gate_ref, cw_ref, cb_ref, wa_ref, ba_ref, wx_ref, bx_ref, lam_ref,
                  o_ref, xbuf, hprev, *, ts, n_group):
    s = pl.program_id(1)
    halo = SUBLANES

    @pl.when(s == 0)
    def _():
        xbuf[0:halo, :] = jnp.zeros((halo, xbuf.shape[1]), F32)
        hprev[...] = jnp.zeros(hprev.shape, F32)

    xbuf[halo:halo + ts, :] = x_ref[...]
    n_sub = ts // SUBLANES
    sub = lax.broadcasted_iota(jnp.int32, (n_sub, SUBLANES, RG_GROUP), 1)

    for gi in range(n_group):
        sl = slice(gi * RG_GROUP, (gi + 1) * RG_GROUP)
        u = cb_ref[:, sl] + cw_ref[CONV_W - 1:CONV_W, sl] * xbuf[halo:halo + ts, sl]
        for back in range(1, CONV_W):
            tap = CONV_W - 1 - back
            u = u + cw_ref[tap:tap + 1, sl] * xbuf[halo - back:halo - back + ts, sl]
        ub = u.astype(BF16)
        r = jax.nn.sigmoid(jnp.dot(ub, wa_ref[gi], preferred_element_type=F32) + ba_ref[:, sl])
        i_gate = jax.nn.sigmoid(jnp.dot(ub, wx_ref[gi], preferred_element_type=F32) + bx_ref[:, sl])
        lam = lam_ref[:, sl]
        neg_softplus = -(jnp.maximum(-lam, 0.0) + jnp.log1p(jnp.exp(-jnp.abs(lam))))
        log_a = (RG_C * neg_softplus) * r
        a = jnp.exp(log_a)
        b = jnp.sqrt(-jnp.tanh(log_a) * (a * a + 1.0)) * (i_gate * u)
        a = a.reshape(n_sub, SUBLANES, RG_GROUP)
        b = b.reshape(n_sub, SUBLANES, RG_GROUP)
        d = 1
        while d < SUBLANES:
            m = sub >= d
            a_s = pltpu.roll(a, d, 1)
            b_s = pltpu.roll(b, d, 1)
            b = jnp.where(m, a * b_s + b, b)
            a = jnp.where(m, a * a_s, a)
            d *= 2
        state = hprev[:, sl]
        groups = []
        for g in range(n_sub):
            hg = a[g] * state + b[g]
            state = hg[SUBLANES - 1:SUBLANES, :]
            groups.append(hg)
        hs = jnp.concatenate(groups, axis=0)
        hprev[:, sl] = state
        o_ref[:, sl] = (hs * jax.nn.gelu(gate_ref[:, sl], approximate=True)).astype(o_ref.dtype)

    xbuf[0:halo, :] = x_ref[ts - halo:ts, :]


def _rglru(rg_x, rg_g, cw, cb, wa, ba, wx, bx, lam, l, *, batch, ts):
    T, C = rg_x.shape
    S = T // batch
    ns = S // ts
    n_group = C // RG_GROUP
    row = pl.BlockSpec((ts, C), lambda b, s: (b * ns + s, 0))
    return pl.pallas_call(
        functools.partial(_rglru_kernel, ts=ts, n_group=n_group),
        grid=(batch, ns),
        in_specs=[row, row,
                  _layer_spec((CONV_W, C), l), _layer_spec((1, C), l),
                  _layer_spec((n_group, RG_GROUP, RG_GROUP), l), _layer_spec((1, C), l),
                  _layer_spec((n_group, RG_GROUP, RG_GROUP), l), _layer_spec((1, C), l),
                  _layer_spec((1, C), l)],
        out_specs=row,
        out_shape=jax.ShapeDtypeStruct((T, C), BF16),
        scratch_shapes=[pltpu.VMEM((ts + SUBLANES, C), F32), pltpu.VMEM((1, C), F32)],
        compiler_params=_cparams(("parallel", "arbitrary")),
        name="rglru",
    )(rg_x, rg_g, cw, cb, wa, ba, wx, bx, lam)


def _sb_kernel(q_ref, k_ref, v_ref, tri_ref, o_ref, *, blk, nblk, hp):
    rowi = lax.broadcasted_iota(jnp.int32, (blk, blk), 0)
    coli = lax.broadcasted_iota(jnp.int32, (blk, blk), 1)
    earlier = coli < rowi
    hd = SB_HEAD_DIM

    def load(ref, j, h):
        r0 = j * blk if isinstance(j, int) else pl.multiple_of(j * blk, blk)
        return ref[pl.ds(r0, blk), h * hd:(h + 1) * hd]

    def block_terms(q, j, h, diag):
        z = lax.dot_general(q, load(k_ref, j, h), _DN_NT, preferred_element_type=F32)
        tail = jnp.log(1.0 + jnp.exp2(-jnp.abs(z))) * LOG2_E
        log_beta = jnp.minimum(z, 0.0) - tail
        log_keep = log_beta - z
        if diag:
            log_keep = jnp.where(earlier, log_keep, 0.0)
        hi = log_keep.astype(BF16)
        lo = (log_keep - hi.astype(F32)).astype(BF16)
        tri = tri_ref[...]
        between = (jnp.dot(hi, tri, preferred_element_type=F32)
                   + jnp.dot(lo, tri, preferred_element_type=F32))
        return log_beta + between, jnp.sum(log_keep, axis=1, keepdims=True)

    def pv(w, j, h):
        return jnp.dot(w.astype(BF16), load(v_ref, j, h), preferred_element_type=F32)

    def diag_block(i, h):
        q = load(q_ref, i, h)
        logw, kept = block_terms(q, i, h, True)
        w = jnp.where(earlier, jnp.exp2(logw), 0.0)
        return q, pv(w, i, h), kept

    def any_live(rems):
        top = rems[0]
        for r in rems[1:]:
            top = jnp.maximum(top, r)
        return jnp.max(top) > SB_DEAD_LOG2

    for h in range(hp):
        _, acc, _ = diag_block(0, h)
        o_ref[0:blk, h * hd:(h + 1) * hd] = acc.astype(o_ref.dtype)

    def q_body(i, carry):
        accs, rems = [], []
        for h in range(hp):
            q, acc, kept_d = diag_block(i, h)
            logw, kept_p = block_terms(q, i - 1, h, False)
            accs.append(acc + pv(jnp.exp2(logw + kept_d), i - 1, h))
            rems.append(kept_d + kept_p)

        def cond(c):
            j, live, _, _ = c
            return jnp.logical_and(j >= 0, live)

        def body(c):
            j, _, accs, rems = c
            new_accs, new_rems = [], []
            for h in range(hp):
                logw, kept = block_terms(load(q_ref, i, h), j, h, False)
                new_accs.append(accs[h] + pv(jnp.exp2(logw + rems[h]), j, h))
                new_rems.append(rems[h] + kept)
            return j - 1, any_live(new_rems), tuple(new_accs), tuple(new_rems)

        _, _, accs, _ = lax.while_loop(cond, body, (i - 2, any_live(rems), tuple(accs), tuple(rems)))
        q0 = pl.multiple_of(i * blk, blk)
        for h in range(hp):
            o_ref[pl.ds(q0, blk), h * hd:(h + 1) * hd] = accs[h].astype(o_ref.dtype)
        return carry

    lax.fori_loop(1, nblk, q_body, 0)


def _sb_attention(q, k, v, *, batch, blk, hp):
    T, W = q.shape
    S = T // batch
    heads = W // SB_HEAD_DIM
    tri = (lax.broadcasted_iota(jnp.int32, (blk, blk), 0)
           > lax.broadcasted_iota(jnp.int32, (blk, blk), 1)).astype(BF16)
    spec = pl.BlockSpec((S, hp * SB_HEAD_DIM), lambda b, h: (b, h))
    return pl.pallas_call(
        functools.partial(_sb_kernel, blk=blk, nblk=S // blk, hp=hp),
        grid=(batch, heads // hp),
        in_specs=[spec, spec, spec, _const_spec((blk, blk))],
        out_specs=spec,
        out_shape=jax.ShapeDtypeStruct((T, W), BF16),
        compiler_params=_cparams(("parallel", "parallel")),
        name="sb_attention",
    )(q, k, v, tri)


def _mla_prep_kernel(cq_ref, ckv_ref, kr_ref, cs_ref, sn_ref, qn_ref, kvn_ref,
                     wq_ref, wqs_ref, wk_ref, wv_ref, q_ref, k_ref, v_ref, *, scale):
    cq = _rms(cq_ref[...], qn_ref[...]).astype(BF16)
    ckv = _rms(ckv_ref[...], kvn_ref[...]).astype(BF16)
    cs = cs_ref[...]
    sn = sn_ref[...]
    kr = kr_ref[...]
    v_ref[...] = jnp.dot(ckv, wv_ref[...], preferred_element_type=F32).astype(BF16)
    for h in range(MLA_HEADS):
        sl = slice(h * MLA_QK_PAD, (h + 1) * MLA_QK_PAD)
        qh = jnp.dot(cq, wq_ref[:, sl], preferred_element_type=F32)
        qh_sw = jnp.dot(cq, wqs_ref[:, sl], preferred_element_type=F32)
        q_ref[:, sl] = ((qh * cs + qh_sw * sn) * scale).astype(BF16)
        kh = jnp.dot(ckv, wk_ref[:, sl], preferred_element_type=F32)
        k_ref[:, sl] = (kh + kr).astype(BF16)


def _mla_prep(cq, ckv, kr, cs, sn, qn, kvn, wq, wqs, wk, wv, l, *, tm):
    T, lora = cq.shape
    qk_w = MLA_HEADS * MLA_QK_PAD
    v_w = MLA_HEADS * MLA_V
    row = lambda width: pl.BlockSpec((tm, width), lambda i: (i, 0))
    return pl.pallas_call(
        functools.partial(_mla_prep_kernel, scale=(MLA_NOPE + MLA_ROPE) ** -0.5 * math.log2(math.e)),
        grid=(T // tm,),
        in_specs=[row(lora), row(lora), row(MLA_QK_PAD), row(MLA_QK_PAD), row(MLA_QK_PAD),
                  _layer_spec((1, lora), l), _layer_spec((1, lora), l),
                  _layer_spec((lora, qk_w), l), _layer_spec((lora, qk_w), l),
                  _layer_spec((lora, qk_w), l), _layer_spec((lora, v_w), l)],
        out_specs=[row(qk_w), row(qk_w), row(v_w)],
        out_shape=[jax.ShapeDtypeStruct((T, qk_w), BF16),
                   jax.ShapeDtypeStruct((T, qk_w), BF16),
                   jax.ShapeDtypeStruct((T, v_w), BF16)],
        compiler_params=_cparams(("parallel",)),
        name="mla_prep",
    )(cq, ckv, kr, cs, sn, qn, kvn, wq, wqs, wk, wv)


def _mla_kernel(q_ref, k_ref, v_ref, o_ref, *, blk, nblk):
    rowc = lax.broadcasted_iota(jnp.int32, (blk, blk), 0) // CHUNK
    colc = lax.broadcasted_iota(jnp.int32, (blk, blk), 1) // CHUNK
    visible = colc <= rowc
    lanes = MLA_V

    def fold(a, op):
        parts = [a[:, c * lanes:(c + 1) * lanes] for c in range(a.shape[1] // lanes)]
        while len(parts) > 1:
            nxt = [op(parts[n], parts[n + 1]) for n in range(0, len(parts) - 1, 2)]
            parts = nxt + parts[len(parts) - len(parts) % 2:]
        return parts[0]

    for i in range(nblk):
        rows = slice(i * blk, (i + 1) * blk)
        q = q_ref[rows, :]
        s_diag = jnp.where(visible, lax.dot_general(q, k_ref[rows, :], _DN_NT, preferred_element_type=F32),
                           -jnp.inf)
        m_lane = fold(s_diag, jnp.maximum)
        if i > 0:
            s_past = lax.dot_general(q, k_ref[0:i * blk, :], _DN_NT, preferred_element_type=F32)
            m_lane = jnp.maximum(m_lane, fold(s_past, jnp.maximum))
        m = jnp.max(m_lane, axis=1, keepdims=True)
        p = jnp.exp2(s_diag - m)
        l_lane = fold(p, jnp.add)
        acc = jnp.dot(p.astype(BF16), v_ref[rows, :], preferred_element_type=F32)
        if i > 0:
            p = jnp.exp2(s_past - m)
            l_lane = l_lane + fold(p, jnp.add)
            acc = acc + jnp.dot(p.astype(BF16), v_ref[0:i * blk, :], preferred_element_type=F32)
        l = jnp.sum(l_lane, axis=1, keepdims=True)
        o_ref[rows, :] = (acc / l).astype(o_ref.dtype)


def _mla_attention(q, k, v, *, batch, blk):
    T = q.shape[0]
    S = T // batch
    qk_spec = pl.BlockSpec((S, MLA_QK_PAD), lambda b, h: (b, h))
    v_spec = pl.BlockSpec((S, MLA_V), lambda b, h: (b, h))
    return pl.pallas_call(
        functools.partial(_mla_kernel, blk=blk, nblk=S // blk),
        grid=(batch, MLA_HEADS),
        in_specs=[qk_spec, qk_spec, v_spec],
        out_specs=v_spec,
        out_shape=jax.ShapeDtypeStruct((T, MLA_HEADS * MLA_V), BF16),
        compiler_params=_cparams(("parallel", "parallel")),
        name="mla_attention",
    )(q, k, v)


def _merge_kernel(x_ref, ya_ref, yb_ref, yc_ref, gate_ref, wa_ref, wb_ref, wc_ref, wo_ref, o_ref, *, d):
    merged = gate_ref[:, 0:d] * jnp.dot(ya_ref[...], wa_ref[...], preferred_element_type=F32)
    merged = merged + gate_ref[:, d:2 * d] * jnp.dot(yb_ref[...], wb_ref[...], preferred_element_type=F32)
    merged = merged + gate_ref[:, 2 * d:3 * d] * jnp.dot(yc_ref[...], wc_ref[...], preferred_element_type=F32)
    o_ref[...] = x_ref[...] + jnp.dot(merged.astype(BF16), wo_ref[...], preferred_element_type=F32)


def _merge(x, ya, yb, yc, gates, wa, wb, wc, wo, l, *, tm):
    T, D = x.shape
    row = lambda width: pl.BlockSpec((tm, width), lambda i: (i, 0))
    wspec = _layer_spec((D, D), l)
    return pl.pallas_call(
        functools.partial(_merge_kernel, d=D),
        grid=(T // tm,),
        in_specs=[row(D), row(D), row(D), row(D), row(3 * D), wspec, wspec, wspec, wspec],
        out_specs=row(D),
        out_shape=jax.ShapeDtypeStruct((T, D), F32),
        compiler_params=_cparams(("parallel",)),
        name="merge",
    )(x, ya, yb, yc, gates, wa, wb, wc, wo)


def _prep_w_in(w_in, d, lora):
    w = w_in.astype(BF16)
    kr_lo = 5 * d + 2 * lora
    kr_hi = kr_lo + MLA_ROPE
    half = MLA_ROPE // 2
    kr = w[:, :, kr_lo:kr_hi]
    lead = w.shape[:2]
    z_nope = jnp.zeros(lead + (MLA_NOPE,), BF16)
    z_tail = jnp.zeros(lead + (MLA_QK_PAD - MLA_NOPE - MLA_ROPE,), BF16)
    kr_pads = jnp.concatenate([z_nope, kr[..., :half], kr[..., half:], z_tail,
                               z_nope, kr[..., half:], kr[..., :half], z_tail], axis=-1)
    return w, w[:, :, kr_hi:], kr_pads


def _prep_w_uq(w_uq):
    L, lora, _ = w_uq.shape
    half = MLA_ROPE // 2
    w = w_uq.reshape(L, lora, MLA_HEADS, MLA_NOPE + MLA_ROPE)
    nope, r1, r2 = w[..., :MLA_NOPE], w[..., MLA_NOPE:MLA_NOPE + half], w[..., MLA_NOPE + half:]
    z_tail = jnp.zeros(w.shape[:3] + (MLA_QK_PAD - MLA_NOPE - MLA_ROPE,), w.dtype)
    plain = jnp.concatenate([nope, r1, r2, z_tail], axis=-1)
    swapped = jnp.concatenate([jnp.zeros_like(nope), r2, r1, z_tail], axis=-1)
    shape = (L, lora, MLA_HEADS * MLA_QK_PAD)
    return plain.reshape(shape).astype(BF16), swapped.reshape(shape).astype(BF16)


def _prep_w_ukv(w_ukv):
    L, lora, _ = w_ukv.shape
    w = w_ukv.reshape(L, lora, MLA_HEADS, MLA_NOPE + MLA_V)
    k_nope, v = w[..., :MLA_NOPE], w[..., MLA_NOPE:]
    z = jnp.zeros(w.shape[:3] + (MLA_QK_PAD - MLA_NOPE,), w.dtype)
    wk = jnp.concatenate([k_nope, z], axis=-1).reshape(L, lora, MLA_HEADS * MLA_QK_PAD)
    wv = v.reshape(L, lora, MLA_HEADS * MLA_V)
    return wk.astype(BF16), wv.astype(BF16)


def _prep_rg_gate(w):
    L, nb, bw, _ = w.shape
    per = RG_GROUP // bw
    w = w.reshape(L, nb // per, per, bw, bw)
    eye = jnp.eye(per, dtype=w.dtype)
    tiles = jnp.einsum('lgpjk,pq->lgpjqk', w, eye)
    return tiles.reshape(L, nb // per, RG_GROUP, RG_GROUP).astype(BF16)


def kernel(x, positions, ffn1_norm, ffn1_w_gate_up, ffn1_w_down, mix_norm, w_in, conv_w, conv_b, rg_w_a, rg_b_a, rg_w_x, rg_b_x, rg_lambda, mla_q_norm, mla_w_uq, mla_kv_norm, mla_w_ukv, w_branch_a, w_branch_b, w_branch_c, w_out, ffn2_norm, ffn2_w_gate_up, ffn2_w_down, final_norm):
    B, S, D = x.shape
    depth = w_in.shape[0]
    lora = mla_q_norm.shape[1]
    T = B * S
    tm = min(512, T)
    tm_in = min(256, T)
    ts = min(256, S)
    blk = min(ATTN_BLOCK, S)
    sb_blk = min(SB_BLOCK, S)

    ffn1_wgu = ffn1_w_gate_up.astype(BF16)
    ffn1_wd = ffn1_w_down.astype(BF16)
    ffn2_wgu = ffn2_w_gate_up.astype(BF16)
    ffn2_wd = ffn2_w_down.astype(BF16)
    w_in_b, w_gate, w_kr = _prep_w_in(w_in, D, lora)
    wq, wq_sw = _prep_w_uq(mla_w_uq)
    wk, wv = _prep_w_ukv(mla_w_ukv)
    rg_wa = _prep_rg_gate(rg_w_a)
    rg_wx = _prep_rg_gate(rg_w_x)
    wba = w_branch_a.astype(BF16)
    wbb = w_branch_b.astype(BF16)
    wbc = w_branch_c.astype(BF16)
    wo = w_out.astype(BF16)

    cs, sn = _rope_tables(positions, tm)
    xf = x.reshape(T, D)
    rows = lambda a: a[:, None, :]
    ffn1_g, mix_g, ffn2_g = rows(ffn1_norm), rows(mix_norm), rows(ffn2_norm)
    cb, ba, bx, lam = rows(conv_b), rows(rg_b_a), rows(rg_b_x), rows(rg_lambda)
    qn, kvn = rows(mla_q_norm), rows(mla_kv_norm)

    for l in range(depth):
        xf = _ffn(xf, ffn1_g, ffn1_wgu, ffn1_wd, l, tm=tm)
        rg_x, rg_g, sb_q, sb_k, sb_v, c_q, c_kv, gates, kr = _inproj(
            xf, mix_g, w_in_b, w_gate, w_kr, cs, sn, l, tm=tm_in, lora=lora)
        y_a = _rglru(rg_x, rg_g, conv_w, cb, rg_wa, ba, rg_wx, bx, lam, l, batch=B, ts=ts)
        y_b = _sb_attention(sb_q, sb_k, sb_v, batch=B, blk=sb_blk, hp=SB_HEADS_PER_STEP)
        mq, mk, mv = _mla_prep(c_q, c_kv, kr, cs, sn, qn, kvn, wq, wq_sw, wk, wv, l, tm=tm)
        y_c = _mla_attention(mq, mk, mv, batch=B, blk=blk)
        xf = _merge(xf, y_a, y_b, y_c, gates, wba, wbb, wbc, wo, l, tm=tm)
        xf = _ffn(xf, ffn2_g, ffn2_wgu, ffn2_wd, l,
                  final_norm[None, :] if l == depth - 1 else None, tm=tm)

    return xf.reshape(B, S, D)
```
